```python
import math
import jax, jax.numpy as jnp
from jax import lax
import numpy as np

D_MODEL = 1024
BATCH = 1
SEQ = 16384
DEPTH = 4

N_A_LAYERS = DEPTH // 2
N_B_LAYERS = DEPTH - N_A_LAYERS
D_FF = 2816
GM_WIDTH = D_MODEL
GM_GROUPS = 8
GM_GROUP_DIM = GM_WIDTH // GM_GROUPS
GM_CHUNK = 128
N_HEADS = 16
HEAD_DIM = D_MODEL // N_HEADS
ATTN_WIDTH = N_HEADS * HEAD_DIM
MOBA_BLOCK = 256
MOBA_TOPK = 3
MOBA_QUERY_CHUNK = 64
REL_BUCKETS = 32
REL_MAX_DIST = 128
DEEPNORM_ALPHA = (2 * DEPTH) ** 0.25
DEEPNORM_BETA = (8 * DEPTH) ** -0.25
LN_EPS = 1e-5

kernel_name = "yoco_gmlp_moba_macaron_deepnorm"


def layer_norm(x, g, b):
    xf = x.astype(jnp.float32)
    mu = jnp.mean(xf, axis=-1, keepdims=True)
    var = jnp.mean(jnp.square(xf - mu), axis=-1, keepdims=True)
    y = (xf - mu) * lax.rsqrt(var + LN_EPS)
    return (y * g.astype(jnp.float32) + b.astype(jnp.float32)).astype(x.dtype)


def swiglu_ffn(x, w_gate, w_up, w_down):
    return (jax.nn.silu(x @ w_gate) * (x @ w_up)) @ w_down


def t5_bucket(dist):
    n = jnp.maximum(dist, 0)
    max_exact = REL_BUCKETS // 2
    nf = jnp.maximum(n, 1).astype(jnp.float32)
    large = max_exact + (jnp.log(nf / max_exact) / math.log(REL_MAX_DIST / max_exact)
                         * (REL_BUCKETS - max_exact)).astype(jnp.int32)
    large = jnp.minimum(large, REL_BUCKETS - 1)
    return jnp.where(n < max_exact, n, large)


def gmlp_mixer(x, w_in, sgu_g, sgu_b, w_s, b_s, w_out):
    B, S, _ = x.shape
    z = jax.nn.gelu(x @ w_in, approximate=False)
    u, v = jnp.split(z, 2, axis=-1)
    v = layer_norm(v, sgu_g, sgu_b)
    v = v.reshape(B, S // GM_CHUNK, GM_CHUNK, GM_GROUPS, GM_GROUP_DIM)
    causal = jnp.tril(jnp.ones((GM_CHUNK, GM_CHUNK), dtype=bool))
    w = jnp.where(causal[None], w_s, 0)
    mixed = jnp.einsum('gts,bcsgd->bctgd', w, v) + b_s.T[None, None, :, :, None]
    y = u * mixed.reshape(B, S, GM_WIDTH)
    return y @ w_out


def moba_core(q, k_blocks, v_blocks, k_mean, rel_bias):
    S_pad = q.shape[0]
    nblk = S_pad // MOBA_BLOCK
    topk = min(MOBA_TOPK, nblk)
    n_chunks = S_pad // MOBA_QUERY_CHUNK
    scale = HEAD_DIM ** -0.5
    hidx = jnp.arange(N_HEADS)[None, :, None]
    blk_ar = jnp.arange(MOBA_BLOCK)

    def chunk_fn(ci):
        start = ci * MOBA_QUERY_CHUNK
        qc = lax.dynamic_slice_in_dim(q, start, MOBA_QUERY_CHUNK, 0)
        qpos = start + jnp.arange(MOBA_QUERY_CHUNK)
        qblk = start // MOBA_BLOCK
        gate = jnp.einsum('chd,hnd->chn', qc, k_mean).astype(jnp.float32)
        past = jnp.arange(nblk) < qblk
        gate = jnp.where(past[None, None, :], gate, -jnp.inf)
        _, idx = lax.top_k(gate, topk)
        valid = idx < qblk
        kg = k_blocks[hidx, idx]
        vg = v_blocks[hidx, idx]
        s_sel = jnp.einsum('chd,chjkd->chjk', qc, kg).astype(jnp.float32) * scale
        kpos_sel = idx[..., None] * MOBA_BLOCK + blk_ar
        bias_sel = rel_bias[hidx[..., None], t5_bucket(qpos[:, None, None, None] - kpos_sel)]
        s_sel = jnp.where(valid[..., None], s_sel + bias_sel.astype(jnp.float32), -jnp.inf)
        k_own = lax.dynamic_index_in_dim(k_blocks, qblk, axis=1, keepdims=False)
        v_own = lax.dynamic_index_in_dim(v_blocks, qblk, axis=1, keepdims=False)
        kpos_own = qblk * MOBA_BLOCK + blk_ar
        d_own = qpos[:, None] - kpos_own[None, :]
        bias_own = jnp.transpose(rel_bias[:, t5_bucket(d_own)], (1, 0, 2))
        s_own = (jnp.einsum('chd,hkd->chk', qc, k_own).astype(jnp.float32) * scale
                 + bias_own.astype(jnp.float32))
        s_own = jnp.where((d_own >= 0)[:, None, :], s_own, -jnp.inf)
        logits = jnp.concatenate(
            [s_sel.reshape(MOBA_QUERY_CHUNK, N_HEADS, topk * MOBA_BLOCK), s_own], axis=-1)
        p = jax.nn.softmax(logits, axis=-1).astype(q.dtype)
        p_sel = p[..., :topk * MOBA_BLOCK]
        p_own = p[..., topk * MOBA_BLOCK:]
        out = (jnp.einsum('chn,chnd->chd', p_sel,
                          vg.reshape(MOBA_QUERY_CHUNK, N_HEADS, topk * MOBA_BLOCK, HEAD_DIM))
               + jnp.einsum('chk,hkd->chd', p_own, v_own))
        return out

    out = lax.map(chunk_fn, jnp.arange(n_chunks))
    return out.reshape(S_pad, N_HEADS, HEAD_DIM)


def moba_mixer(x, w_q, w_o, k_blocks, v_blocks, k_mean, rel_bias):
    B, S, _ = x.shape
    S_pad = k_blocks.shape[2] * MOBA_BLOCK
    q = (x @ w_q).reshape(B, S, N_HEADS, HEAD_DIM)
    q = jnp.pad(q, ((0, 0), (0, S_pad - S), (0, 0), (0, 0)))
    o = jax.vmap(moba_core, in_axes=(0, 0, 0, 0, None))(q, k_blocks, v_blocks, k_mean, rel_bias)
    return o[:, :S].reshape(B, S, ATTN_WIDTH) @ w_o


def shared_kv(x, w_k, w_v):
    B, S, _ = x.shape
    S_pad = -(-S // MOBA_BLOCK) * MOBA_BLOCK
    nblk = S_pad // MOBA_BLOCK
    pad = ((0, 0), (0, S_pad - S), (0, 0), (0, 0))
    k = jnp.pad((x @ w_k).reshape(B, S, N_HEADS, HEAD_DIM), pad)
    v = jnp.pad((x @ w_v).reshape(B, S, N_HEADS, HEAD_DIM), pad)
    k_blocks = jnp.transpose(k.reshape(B, nblk, MOBA_BLOCK, N_HEADS, HEAD_DIM), (0, 3, 1, 2, 4))
    v_blocks = jnp.transpose(v.reshape(B, nblk, MOBA_BLOCK, N_HEADS, HEAD_DIM), (0, 3, 1, 2, 4))
    k_mean = jnp.mean(k_blocks.astype(jnp.float32), axis=3).astype(k.dtype)
    return k_blocks, v_blocks, k_mean


def setup_inputs(seed: int = 0) -> dict:
    key = jax.random.key(seed)
    ks = jax.random.split(key, 20)
    f32 = jnp.float32
    nrm = lambda k, shape, s: jax.random.normal(k, shape, f32) * s
    return {
        "x": nrm(ks[0], (BATCH, SEQ, D_MODEL), 1.0),
        "ln_g": 1.0 + nrm(ks[1], (DEPTH, 3, D_MODEL), 0.02),
        "ln_b": nrm(ks[2], (DEPTH, 3, D_MODEL), 0.02),
        "ffn_w_gate": nrm(ks[3], (DEPTH, 2, D_MODEL, D_FF), D_MODEL ** -0.5),
        "ffn_w_up": nrm(ks[4], (DEPTH, 2, D_MODEL, D_FF), D_MODEL ** -0.5),
        "ffn_w_down": nrm(ks[5], (DEPTH, 2, D_FF, D_MODEL), D_FF ** -0.5 * DEEPNORM_BETA),
        "gm_w_in": nrm(ks[6], (N_A_LAYERS, D_MODEL, 2 * GM_WIDTH), D_MODEL ** -0.5),
        "gm_sgu_g": 1.0 + nrm(ks[7], (N_A_LAYERS, GM_WIDTH), 0.02),
        "gm_sgu_b": nrm(ks[8], (N_A_LAYERS, GM_WIDTH), 0.02),
        "gm_w_s": nrm(ks[9], (N_A_LAYERS, GM_GROUPS, GM_CHUNK, GM_CHUNK), GM_CHUNK ** -0.5),
        "gm_b_s": 1.0 + nrm(ks[10], (N_A_LAYERS, GM_GROUPS, GM_CHUNK), 0.02),
        "gm_w_out": nrm(ks[11], (N_A_LAYERS, GM_WIDTH, D_MODEL), GM_WIDTH ** -0.5 * DEEPNORM_BETA),
        "attn_w_q": nrm(ks[12], (N_B_LAYERS, D_MODEL, ATTN_WIDTH), D_MODEL ** -0.5),
        "attn_w_o": nrm(ks[13], (N_B_LAYERS, ATTN_WIDTH, D_MODEL), ATTN_WIDTH ** -0.5 * DEEPNORM_BETA),
        "w_k_shared": nrm(ks[14], (D_MODEL, ATTN_WIDTH), D_MODEL ** -0.5),
        "w_v_shared": nrm(ks[15], (D_MODEL, ATTN_WIDTH), D_MODEL ** -0.5 * DEEPNORM_BETA),
        "rel_bias": nrm(ks[16], (N_HEADS, REL_BUCKETS), 0.1),
    }


def reference(x, ln_g, ln_b, ffn_w_gate, ffn_w_up, ffn_w_down,
              gm_w_in, gm_sgu_g, gm_sgu_b, gm_w_s, gm_b_s, gm_w_out,
              attn_w_q, attn_w_o, w_k_shared, w_v_shared, rel_bias):
    k_blocks = v_blocks = k_mean = None
    for layer in range(DEPTH):
        h = swiglu_ffn(x, ffn_w_gate[layer, 0], ffn_w_up[layer, 0], ffn_w_down[layer, 0])
        x = layer_norm(DEEPNORM_ALPHA * x + 0.5 * h, ln_g[layer, 0], ln_b[layer, 0])
        if layer < N_A_LAYERS:
            a = layer
            mix = gmlp_mixer(x, gm_w_in[a], gm_sgu_g[a], gm_sgu_b[a], gm_w_s[a], gm_b_s[a], gm_w_out[a])
        else:
            j = layer - N_A_LAYERS
            mix = moba_mixer(x, attn_w_q[j], attn_w_o[j], k_blocks, v_blocks, k_mean, rel_bias)
        x = layer_norm(DEEPNORM_ALPHA * x + mix, ln_g[layer, 1], ln_b[layer, 1])
        h = swiglu_ffn(x, ffn_w_gate[layer, 1], ffn_w_up[layer, 1], ffn_w_down[layer, 1])
        x = layer_norm(DEEPNORM_ALPHA * x + 0.5 * h, ln_g[layer, 2], ln_b[layer, 2])
        if layer == N_A_LAYERS - 1:
            k_blocks, v_blocks, k_mean = shared_kv(x, w_k_shared, w_v_shared)
    return x
```

```python
import functools
import math

import jax
import jax.numpy as jnp
import numpy as np
from jax import lax
from jax.experimental import pallas as pl
from jax.experimental.pallas import tpu as pltpu

DEPTH = 4
N_A_LAYERS = DEPTH // 2
GM_GROUPS = 8
GM_CHUNK = 128
N_HEADS = 16
HEAD_DIM = 64
MOBA_BLOCK = 256
MOBA_TOPK = 3
REL_BUCKETS = 32
REL_MAX_DIST = 128
DEEPNORM_ALPHA = (2 * DEPTH) ** 0.25
LN_EPS = 1e-5

HEADS_PER_STEP = 2
PAIR_W = HEADS_PER_STEP * HEAD_DIM
NEG_BIG = -1e30
POS_BIG = 1e30

FFN_ROWS = 512
FFN_CHUNK = 256
MIX_ROWS = 512
VMEM_LIMIT = 56 * 1024 * 1024

F32 = jnp.float32
BF16 = jnp.bfloat16
_NT = (((1,), (1,)), ((), ()))


def _params(n_axes=1):
    return pltpu.CompilerParams(dimension_semantics=("arbitrary",) * n_axes,
                                vmem_limit_bytes=VMEM_LIMIT)


def _const_spec(shape):
    zeros = (0,) * len(shape)
    return pl.BlockSpec(shape, lambda *_: zeros, pipeline_mode=pl.Buffered(1))


def _layer_norm(y, g, b):
    mu = jnp.mean(y, axis=-1, keepdims=True)
    d = y - mu
    var = jnp.mean(d * d, axis=-1, keepdims=True)
    return d * lax.rsqrt(var + LN_EPS) * g + b


def _dot(a, b):
    return jnp.dot(a, b, preferred_element_type=F32)


def _ffn_kernel(x_ref, wg_ref, wu_ref, wd_ref, g_ref, b_ref, o_ref, acc_ref):
    x = x_ref[...]
    xb = x.astype(BF16)
    acc_ref[...] = jnp.zeros_like(acc_ref)

    def body(c, carry):
        h1 = _dot(xb, wg_ref[c])
        h2 = _dot(xb, wu_ref[c])
        a = (h1 * jax.nn.sigmoid(h1)) * h2
        acc_ref[...] += _dot(a.astype(BF16), wd_ref[c])
        return carry

    lax.fori_loop(0, wg_ref.shape[0], body, 0)
    y = DEEPNORM_ALPHA * x + 0.5 * acc_ref[...]
    o_ref[...] = _layer_norm(y, g_ref[...], b_ref[...])


def _ffn_ln(x, w_gate, w_up, w_down, g, b):
    s, d = x.shape
    f = w_gate.shape[1]
    fc = FFN_CHUNK if f % FFN_CHUNK == 0 else f
    nc = f // fc
    tm = min(FFN_ROWS, s)
    wg = w_gate.reshape(d, nc, fc).transpose(1, 0, 2).astype(BF16)
    wu = w_up.reshape(d, nc, fc).transpose(1, 0, 2).astype(BF16)
    wd = w_down.reshape(nc, fc, d).astype(BF16)
    row = pl.BlockSpec((tm, d), lambda i: (i, 0))
    return pl.pallas_call(
        _ffn_kernel,
        grid=(s // tm,),
        in_specs=[row, _const_spec((nc, d, fc)), _const_spec((nc, d, fc)), _const_spec((nc, fc, d)),
                  _const_spec((1, d)), _const_spec((1, d))],
        out_specs=row,
        out_shape=jax.ShapeDtypeStruct((s, d), F32),
        scratch_shapes=[pltpu.VMEM((tm, d), F32)],
        compiler_params=_params(),
        name="ffn_ln",
    )(x, wg, wu, wd, g.reshape(1, d), b.reshape(1, d))


def _gmlp_kernel(x_ref, win_ref, sg_ref, sb_ref, ws_ref, bs_ref, wout_ref, g_ref, b_ref, o_ref,
                 v_scr, mix_scr):
    tm = x_ref.shape[0]
    w = wout_ref.shape[0]
    gd = w // GM_GROUPS
    nch = tm // GM_CHUNK
    x = x_ref[...]
    z = _dot(x.astype(BF16), win_ref[...])
    z = 0.5 * z * (1.0 + lax.erf(z * np.float32(math.sqrt(0.5))))
    u = z[:, :w]
    v_scr[...] = _layer_norm(z[:, w:], sg_ref[...], sb_ref[...]).astype(BF16)
    t_idx = lax.broadcasted_iota(jnp.int32, (GM_CHUNK, GM_CHUNK), 0)
    s_idx = lax.broadcasted_iota(jnp.int32, (GM_CHUNK, GM_CHUNK), 1)
    for grp in range(GM_GROUPS):
        cols = slice(grp * gd, (grp + 1) * gd)
        w_tril = jnp.where(t_idx >= s_idx, ws_ref[grp], 0.0).astype(BF16)
        vg = jnp.concatenate([v_scr[c * GM_CHUNK:(c + 1) * GM_CHUNK, cols] for c in range(nch)], axis=1)
        mg = _dot(w_tril, vg)
        for c in range(nch):
            mix_scr[c * GM_CHUNK:(c + 1) * GM_CHUNK, cols] = mg[:, c * gd:(c + 1) * gd] + bs_ref[:, cols]
    y = (u * mix_scr[...]).astype(BF16)
    out = DEEPNORM_ALPHA * x + _dot(y, wout_ref[...])
    o_ref[...] = _layer_norm(out, g_ref[...], b_ref[...])


def _gmlp_ln(x, w_in, sgu_g, sgu_b, w_s, b_s, w_out, g, b):
    s, d = x.shape
    w = w_out.shape[0]
    tm = min(MIX_ROWS, s)
    bs_full = jnp.repeat(b_s.T, w // GM_GROUPS, axis=1)
    row = pl.BlockSpec((tm, d), lambda i: (i, 0))
    return pl.pallas_call(
        _gmlp_kernel,
        grid=(s // tm,),
        in_specs=[row, _const_spec((d, 2 * w)), _const_spec((1, w)), _const_spec((1, w)),
                  _const_spec((GM_GROUPS, GM_CHUNK, GM_CHUNK)), _const_spec((GM_CHUNK, w)),
                  _const_spec((w, d)), _const_spec((1, d)), _const_spec((1, d))],
        out_specs=row,
        out_shape=jax.ShapeDtypeStruct((s, d), F32),
        scratch_shapes=[pltpu.VMEM((tm, w), BF16), pltpu.VMEM((tm, w), F32)],
        compiler_params=_params(),
        name="gmlp_ln",
    )(x, w_in.astype(BF16), sgu_g.reshape(1, w), sgu_b.reshape(1, w), w_s, bs_full,
      w_out.astype(BF16), g.reshape(1, d), b.reshape(1, d))


def _kv_kernel(x_ref, wk_ref, wvt_ref, k_ref, vt_ref, km_ref):
    tm = x_ref.shape[0]
    xb = x_ref[...].astype(BF16)
    k = _dot(xb, wk_ref[...])
    vt = lax.dot_general(wvt_ref[...], xb, _NT, preferred_element_type=F32)
    kb = k.astype(BF16)
    vtb = vt.astype(BF16)
    for blk in range(tm // MOBA_BLOCK):
        rows = slice(blk * MOBA_BLOCK, (blk + 1) * MOBA_BLOCK)
        km_ref[blk] = jnp.mean(k[rows], axis=0, keepdims=True)
        for p in range(k_ref.shape[0]):
            k_ref[p, blk] = kb[rows, p * PAIR_W:(p + 1) * PAIR_W]
        for h in range(vt_ref.shape[0]):
            vt_ref[h, blk] = vtb[h * HEAD_DIM:(h + 1) * HEAD_DIM, rows]


def _shared_kv(x, w_k, w_v):
    s, d = x.shape
    aw = w_k.shape[1]
    nblk = s // MOBA_BLOCK
    tm = min(2 * MOBA_BLOCK, s)
    r = tm // MOBA_BLOCK
    npair = aw // PAIR_W
    k4, vt4, km = pl.pallas_call(
        _kv_kernel,
        grid=(s // tm,),
        in_specs=[pl.BlockSpec((tm, d), lambda i: (i, 0)), _const_spec((d, aw)), _const_spec((aw, d))],
        out_specs=[pl.BlockSpec((npair, r, MOBA_BLOCK, PAIR_W), lambda i: (0, i, 0, 0)),
                   pl.BlockSpec((N_HEADS, r, HEAD_DIM, MOBA_BLOCK), lambda i: (0, i, 0, 0)),
                   pl.BlockSpec((r, 1, aw), lambda i: (i, 0, 0))],
        out_shape=[jax.ShapeDtypeStruct((npair, nblk, MOBA_BLOCK, PAIR_W), BF16),
                   jax.ShapeDtypeStruct((N_HEADS, nblk, HEAD_DIM, MOBA_BLOCK), BF16),
                   jax.ShapeDtypeStruct((nblk, 1, aw), F32)],
        compiler_params=_params(),
        name="shared_kv",
    )(x, w_k.astype(BF16), w_v.T.astype(BF16))
    return k4, vt4, km.reshape(nblk, aw).astype(BF16)


def _t5_bucket_np(dist):
    n = np.maximum(dist, 0)
    max_exact = REL_BUCKETS // 2
    nf = np.maximum(n, 1).astype(np.float32)
    large = max_exact + (np.log(nf / np.float32(max_exact)) / np.float32(math.log(REL_MAX_DIST / max_exact))
                         * np.float32(REL_BUCKETS - max_exact)).astype(np.int32)
    large = np.minimum(large, REL_BUCKETS - 1)
    return np.where(n < max_exact, n, large).astype(np.int32)


def _bucket_tiles(seq):
    kpos = np.arange(MOBA_BLOCK)[:, None]
    qpos = np.arange(MOBA_BLOCK)[None, :]
    own = _t5_bucket_np(qpos - kpos)
    prev = _t5_bucket_np(qpos - kpos + MOBA_BLOCK)
    far = _t5_bucket_np(np.arange(MOBA_BLOCK + 1, max(seq, MOBA_BLOCK + 2)))
    assert (far == far[0]).all(), "bias must be constant beyond the previous block"
    return own, prev, int(far[0])


def _q_kernel(x_ref, wq_ref, q_ref):
    q = _dot(x_ref[...].astype(BF16), wq_ref[...])
    q_ref[...] = (q * np.float32(HEAD_DIM ** -0.5)).astype(BF16)


def _q_proj(x, w_q):
    s, d = x.shape
    aw = w_q.shape[1]
    tm = min(MIX_ROWS, s)
    return pl.pallas_call(
        _q_kernel,
        grid=(s // tm,),
        in_specs=[pl.BlockSpec((tm, d), lambda i: (i, 0)), _const_spec((d, aw))],
        out_specs=pl.BlockSpec((tm, aw), lambda i: (i, 0)),
        out_shape=jax.ShapeDtypeStruct((s, aw), BF16),
        compiler_params=_params(),
        name="q_proj",
    )(x, w_q.astype(BF16))


def _moba_kernel(rb_ref, q_ref, k_ref, vt_ref, km_ref, bo_ref, bp_ref, o_ref, bias_scr, sel_scr,
                 *, far_bucket):
    hp = pl.program_id(0)
    i = pl.program_id(1)
    tq = q_ref.shape[0]
    nblk = km_ref.shape[0]

    @pl.when(i == 0)
    def _():
        for hh in range(HEADS_PER_STEP):
            h = hp * HEADS_PER_STEP + hh
            for kind, bkt_ref in enumerate((bo_ref, bp_ref)):
                bkt = bkt_ref[...]
                tile = jnp.zeros(bkt.shape, F32)
                for bucket in range(REL_BUCKETS):
                    tile = jnp.where(bkt == bucket, rb_ref[h, bucket], tile)
                bias_scr[hh, kind] = tile

    qf = q_ref[...].astype(F32)
    lane = lax.broadcasted_iota(jnp.int32, qf.shape, 1)
    blk_row = lax.broadcasted_iota(jnp.int32, (nblk, tq), 0)
    kpos = lax.broadcasted_iota(jnp.int32, (MOBA_BLOCK, tq), 0)
    qpos = lax.broadcasted_iota(jnp.int32, (MOBA_BLOCK, tq), 1)
    j_prev = jnp.maximum(i - 1, 0)
    outs = []
    for hh in range(HEADS_PER_STEP):
        h = hp * HEADS_PER_STEP + hh
        qm = jnp.where((lane >= hh * HEAD_DIM) & (lane < (hh + 1) * HEAD_DIM), qf, 0.0).astype(BF16)
        c_far = rb_ref[h, far_bucket]

        gate = lax.dot_general(km_ref[...], qm, _NT, preferred_element_type=F32)
        gate = jnp.where(blk_row < i, gate, -jnp.inf)
        sel = jnp.zeros((nblk, tq), F32)
        for _ in range(MOBA_TOPK):
            top = jnp.max(gate, axis=0, keepdims=True)
            first = jnp.min(jnp.where(gate == top, blk_row, nblk), axis=0, keepdims=True)
            pick = (blk_row == first) & (top > -jnp.inf)
            sel = jnp.where(pick, 1.0, sel)
            gate = jnp.where(pick, -jnp.inf, gate)
        sel_scr[...] = sel

        def attend(j, logits_fn, on, carry):
            m, l, acc = carry
            logits = logits_fn(lax.dot_general(k_ref[j], qm, _NT, preferred_element_type=F32))
            blk_max = jnp.max(logits, axis=0, keepdims=True)
            m_new = jnp.maximum(m, jnp.where(on, blk_max, NEG_BIG))
            p = jnp.exp(logits - jnp.where(on, m_new, POS_BIG))
            scale = jnp.exp(m - m_new)
            pv = _dot(vt_ref[hh, j], p.astype(BF16))
            return m_new, l * scale + jnp.sum(p, axis=0, keepdims=True), acc * scale + pv

        def far_body(j, carry):
            on = sel_scr[pl.ds(j, 1), :] > 0.0
            return attend(j, lambda s: s + c_far, on, carry)

        carry = (jnp.full((1, tq), NEG_BIG, F32), jnp.zeros((1, tq), F32), jnp.zeros((HEAD_DIM, tq), F32))
        carry = lax.fori_loop(0, j_prev, far_body, carry)
        on_prev = sel_scr[pl.ds(j_prev, 1), :] > 0.0
        carry = attend(j_prev, lambda s: s + bias_scr[hh, 1], on_prev, carry)
        carry = attend(i, lambda s: jnp.where(qpos >= kpos, s + bias_scr[hh, 0], NEG_BIG),
                       jnp.full((1, tq), True), carry)
        _, l, acc = carry
        outs.append(acc / l)
    o_ref[...] = jnp.concatenate(outs, axis=0).T.astype(o_ref.dtype)


def _moba_attention(q, k4, vt4, km, rel_bias):
    s, aw = q.shape
    nblk = s // MOBA_BLOCK
    npair = aw // PAIR_W
    own, prev, far_bucket = _bucket_tiles(s)
    tile = pl.BlockSpec((MOBA_BLOCK, PAIR_W), lambda hp, i: (i, hp))
    return pl.pallas_call(
        functools.partial(_moba_kernel, far_bucket=far_bucket),
        grid=(npair, nblk),
        in_specs=[pl.BlockSpec(memory_space=pltpu.SMEM),
                  tile,
                  pl.BlockSpec((None, nblk, MOBA_BLOCK, PAIR_W), lambda hp, i: (hp, 0, 0, 0)),
                  pl.BlockSpec((HEADS_PER_STEP, nblk, HEAD_DIM, MOBA_BLOCK), lambda hp, i: (hp, 0, 0, 0)),
                  pl.BlockSpec((nblk, PAIR_W), lambda hp, i: (0, hp)),
                  pl.BlockSpec((MOBA_BLOCK, MOBA_BLOCK), lambda hp, i: (0, 0)),
                  pl.BlockSpec((MOBA_BLOCK, MOBA_BLOCK), lambda hp, i: (0, 0))],
        out_specs=tile,
        out_shape=jax.ShapeDtypeStruct((s, aw), BF16),
        scratch_shapes=[pltpu.VMEM((HEADS_PER_STEP, 2, MOBA_BLOCK, MOBA_BLOCK), F32),
                        pltpu.VMEM((nblk, MOBA_BLOCK), F32)],
        compiler_params=_params(2),
        name="moba_attn",
    )(rel_bias, q, k4, vt4, km, jnp.asarray(own), jnp.asarray(prev))


def _proj_kernel(a_ref, x_ref, w_ref, g_ref, b_ref, o_ref):
    y = DEEPNORM_ALPHA * x_ref[...] + _dot(a_ref[...], w_ref[...])
    o_ref[...] = _layer_norm(y, g_ref[...], b_ref[...])


def _proj_ln(a, x, w, g, b):
    s, d = x.shape
    aw = a.shape[1]
    tm = min(MIX_ROWS, s)
    return pl.pallas_call(
        _proj_kernel,
        grid=(s // tm,),
        in_specs=[pl.BlockSpec((tm, aw), lambda i: (i, 0)), pl.BlockSpec((tm, d), lambda i: (i, 0)),
                  _const_spec((aw, d)), _const_spec((1, d)), _const_spec((1, d))],
        out_specs=pl.BlockSpec((tm, d), lambda i: (i, 0)),
        out_shape=jax.ShapeDtypeStruct((s, d), F32),
        compiler_params=_params(),
        name="attn_out_ln",
    )(a, x, w.astype(BF16), g.reshape(1, d), b.reshape(1, d))


def kernel(x, ln_g, ln_b, ffn_w_gate, ffn_w_up, ffn_w_down, gm_w_in, gm_sgu_g, gm_sgu_b, gm_w_s, gm_b_s, gm_w_out, attn_w_q, attn_w_o, w_k_shared, w_v_shared, rel_bias):
    batch, seq, d = x.shape
    assert seq % MOBA_BLOCK == 0 and w_k_shared.shape[1] == N_HEADS * HEAD_DIM
    outs = []
    for bi in range(batch):
        h = x[bi]
        kv = None
        for layer in range(DEPTH):
            h = _ffn_ln(h, ffn_w_gate[layer, 0], ffn_w_up[layer, 0], ffn_w_down[layer, 0],
                        ln_g[layer, 0], ln_b[layer, 0])
            if layer < N_A_LAYERS:
                a = layer
                h = _gmlp_ln(h, gm_w_in[a], gm_sgu_g[a], gm_sgu_b[a], gm_w_s[a], gm_b_s[a], gm_w_out[a],
                             ln_g[layer, 1], ln_b[layer, 1])
            else:
                j = layer - N_A_LAYERS
                o = _moba_attention(_q_proj(h, attn_w_q[j]), *kv, rel_bias)
                h = _proj_ln(o, h, attn_w_o[j], ln_g[layer, 1], ln_b[layer, 1])
            h = _ffn_ln(h, ffn_w_gate[layer, 1], ffn_w_up[layer, 1], ffn_w_down[layer, 1],
                        ln_g[layer, 2], ln_b[layer, 2])
            if layer == N_A_LAYERS - 1:
                kv = _shared_kv(h, w_k_shared, w_v_shared)
        outs.append(h)
    return jnp.stack(outs)
```

```python
import functools
import math

import jax
import jax.numpy as jnp
import numpy as np
from jax import lax
from jax.experimental import pallas as pl
from jax.experimental.pallas import tpu as pltpu

DEPTH = 4
N_A_LAYERS = DEPTH // 2
GM_GROUPS = 8
GM_CHUNK = 128
N_HEADS = 16
HEAD_DIM = 64
MOBA_BLOCK = 256
MOBA_TOPK = 3
REL_BUCKETS = 32
REL_MAX_DIST = 128
DEEPNORM_ALPHA = (2 * DEPTH) ** 0.25
LN_EPS = 1e-5

HEADS_PER_STEP = 2
PAIR_W = HEADS_PER_STEP * HEAD_DIM
VT_ROWS = HEAD_DIM + 16
FAR_STEP = 4
LOG2E = math.log2(math.e)
NEG_BIG = -1e30
POS_BIG = 1e30

FFN_ROWS = 512
FFN_CHUNK = 256
MIX_ROWS = 512
VMEM_LIMIT = 56 * 1024 * 1024

F32 = jnp.float32
BF16 = jnp.bfloat16
_NT = (((1,), (1,)), ((), ()))


def _params(n_axes=1):
    return pltpu.CompilerParams(dimension_semantics=("arbitrary",) * n_axes,
                                vmem_limit_bytes=VMEM_LIMIT)


def _const_spec(shape):
    zeros = (0,) * len(shape)
    return pl.BlockSpec(shape, lambda *_: zeros, pipeline_mode=pl.Buffered(1))


def _layer_norm(y, g, b):
    mu = jnp.mean(y, axis=-1, keepdims=True)
    d = y - mu
    var = jnp.mean(d * d, axis=-1, keepdims=True)
    return d * lax.rsqrt(var + LN_EPS) * g + b


def _dot(a, b):
    return jnp.dot(a, b, preferred_element_type=F32)


def _ffn_kernel(x_ref, wg_ref, wu_ref, wd_ref, g_ref, b_ref, o_ref, acc_ref):
    x = x_ref[...]
    xb = x.astype(BF16)
    acc_ref[...] = jnp.zeros_like(acc_ref)

    def body(c, carry):
        h1 = _dot(xb, wg_ref[c])
        h2 = _dot(xb, wu_ref[c])
        a = (h1 * jax.nn.sigmoid(h1)) * h2
        acc_ref[...] += _dot(a.astype(BF16), wd_ref[c])
        return carry

    lax.fori_loop(0, wg_ref.shape[0], body, 0)
    y = DEEPNORM_ALPHA * x + 0.5 * acc_ref[...]
    o_ref[...] = _layer_norm(y, g_ref[...], b_ref[...])


def _ffn_ln(x, w_gate, w_up, w_down, g, b):
    s, d = x.shape
    f = w_gate.shape[1]
    fc = FFN_CHUNK if f % FFN_CHUNK == 0 else f
    nc = f // fc
    tm = min(FFN_ROWS, s)
    wg = w_gate.reshape(d, nc, fc).transpose(1, 0, 2).astype(BF16)
    wu = w_up.reshape(d, nc, fc).transpose(1, 0, 2).astype(BF16)
    wd = w_down.reshape(nc, fc, d).astype(BF16)
    row = pl.BlockSpec((tm, d), lambda i: (i, 0))
    return pl.pallas_call(
        _ffn_kernel,
        grid=(s // tm,),
        in_specs=[row, _const_spec((nc, d, fc)), _const_spec((nc, d, fc)), _const_spec((nc, fc, d)),
                  _const_spec((1, d)), _const_spec((1, d))],
        out_specs=row,
        out_shape=jax.ShapeDtypeStruct((s, d), F32),
        scratch_shapes=[pltpu.VMEM((tm, d), F32)],
        compiler_params=_params(),
        name="ffn_ln",
    )(x, wg, wu, wd, g.reshape(1, d), b.reshape(1, d))


def _gmlp_kernel(x_ref, win_ref, sg_ref, sb_ref, ws_ref, bs_ref, wout_ref, g_ref, b_ref, o_ref,
                 v_scr, mix_scr):
    tm = x_ref.shape[0]
    w = wout_ref.shape[0]
    gd = w // GM_GROUPS
    nch = tm // GM_CHUNK
    x = x_ref[...]
    z = _dot(x.astype(BF16), win_ref[...])
    z = 0.5 * z * (1.0 + lax.erf(z * np.float32(math.sqrt(0.5))))
    u = z[:, :w]
    v_scr[...] = _layer_norm(z[:, w:], sg_ref[...], sb_ref[...]).astype(BF16)
    t_idx = lax.broadcasted_iota(jnp.int32, (GM_CHUNK, GM_CHUNK), 0)
    s_idx = lax.broadcasted_iota(jnp.int32, (GM_CHUNK, GM_CHUNK), 1)
    for grp in range(GM_GROUPS):
        cols = slice(grp * gd, (grp + 1) * gd)
        w_tril = jnp.where(t_idx >= s_idx, ws_ref[grp], 0.0).astype(BF16)
        vg = jnp.concatenate([v_scr[c * GM_CHUNK:(c + 1) * GM_CHUNK, cols] for c in range(nch)], axis=1)
        mg = _dot(w_tril, vg)
        for c in range(nch):
            mix_scr[c * GM_CHUNK:(c + 1) * GM_CHUNK, cols] = mg[:, c * gd:(c + 1) * gd] + bs_ref[:, cols]
    y = (u * mix_scr[...]).astype(BF16)
    out = DEEPNORM_ALPHA * x + _dot(y, wout_ref[...])
    o_ref[...] = _layer_norm(out, g_ref[...], b_ref[...])


def _gmlp_ln(x, w_in, sgu_g, sgu_b, w_s, b_s, w_out, g, b):
    s, d = x.shape
    w = w_out.shape[0]
    tm = min(MIX_ROWS, s)
    bs_full = jnp.repeat(b_s.T, w // GM_GROUPS, axis=1)
    row = pl.BlockSpec((tm, d), lambda i: (i, 0))
    return pl.pallas_call(
        _gmlp_kernel,
        grid=(s // tm,),
        in_specs=[row, _const_spec((d, 2 * w)), _const_spec((1, w)), _const_spec((1, w)),
                  _const_spec((GM_GROUPS, GM_CHUNK, GM_CHUNK)), _const_spec((GM_CHUNK, w)),
                  _const_spec((w, d)), _const_spec((1, d)), _const_spec((1, d))],
        out_specs=row,
        out_shape=jax.ShapeDtypeStruct((s, d), F32),
        scratch_shapes=[pltpu.VMEM((tm, w), BF16), pltpu.VMEM((tm, w), F32)],
        compiler_params=_params(),
        name="gmlp_ln",
    )(x, w_in.astype(BF16), sgu_g.reshape(1, w), sgu_b.reshape(1, w), w_s, bs_full,
      w_out.astype(BF16), g.reshape(1, d), b.reshape(1, d))


def _kv_kernel(x_ref, wk_ref, wvt_ref, k_ref, vt_ref, km_ref):
    tm = x_ref.shape[0]
    xb = x_ref[...].astype(BF16)
    k = _dot(xb, wk_ref[...])
    vt = lax.dot_general(wvt_ref[...], xb, _NT, preferred_element_type=F32)
    kb = k.astype(BF16)
    vtb = vt.astype(BF16)
    pad_row = lax.broadcasted_iota(jnp.int32, (VT_ROWS - HEAD_DIM, MOBA_BLOCK), 0)
    pad = jnp.where(pad_row == 0, 1.0, 0.0).astype(BF16)
    for blk in range(tm // MOBA_BLOCK):
        rows = slice(blk * MOBA_BLOCK, (blk + 1) * MOBA_BLOCK)
        km_ref[blk] = jnp.mean(k[rows], axis=0, keepdims=True)
        for p in range(k_ref.shape[0]):
            k_ref[p, blk] = kb[rows, p * PAIR_W:(p + 1) * PAIR_W]
        for h in range(vt_ref.shape[0]):
            vt_ref[h, blk] = jnp.concatenate([vtb[h * HEAD_DIM:(h + 1) * HEAD_DIM, rows], pad], axis=0)


def _shared_kv(x, w_k, w_v):
    s, d = x.shape
    aw = w_k.shape[1]
    nblk = s // MOBA_BLOCK
    tm = min(2 * MOBA_BLOCK, s)
    r = tm // MOBA_BLOCK
    npair = aw // PAIR_W
    k4, vt4, km = pl.pallas_call(
        _kv_kernel,
        grid=(s // tm,),
        in_specs=[pl.BlockSpec((tm, d), lambda i: (i, 0)), _const_spec((d, aw)), _const_spec((aw, d))],
        out_specs=[pl.BlockSpec((npair, r, MOBA_BLOCK, PAIR_W), lambda i: (0, i, 0, 0)),
                   pl.BlockSpec((N_HEADS, r, VT_ROWS, MOBA_BLOCK), lambda i: (0, i, 0, 0)),
                   pl.BlockSpec((r, 1, aw), lambda i: (i, 0, 0))],
        out_shape=[jax.ShapeDtypeStruct((npair, nblk, MOBA_BLOCK, PAIR_W), BF16),
                   jax.ShapeDtypeStruct((N_HEADS, nblk, VT_ROWS, MOBA_BLOCK), BF16),
                   jax.ShapeDtypeStruct((nblk, 1, aw), F32)],
        compiler_params=_params(),
        name="shared_kv",
    )(x, w_k.astype(BF16), w_v.T.astype(BF16))
    return k4, vt4, km.reshape(nblk, aw).astype(BF16)


def _t5_bucket_np(dist):
    n = np.maximum(dist, 0)
    max_exact = REL_BUCKETS // 2
    nf = np.maximum(n, 1).astype(np.float32)
    large = max_exact + (np.log(nf / np.float32(max_exact)) / np.float32(math.log(REL_MAX_DIST / max_exact))
                         * np.float32(REL_BUCKETS - max_exact)).astype(np.int32)
    large = np.minimum(large, REL_BUCKETS - 1)
    return np.where(n < max_exact, n, large).astype(np.int32)


def _bucket_tiles(seq):
    kpos = np.arange(MOBA_BLOCK)[:, None]
    qpos = np.arange(MOBA_BLOCK)[None, :]
    own = _t5_bucket_np(qpos - kpos)
    prev = _t5_bucket_np(qpos - kpos + MOBA_BLOCK)
    far = _t5_bucket_np(np.arange(MOBA_BLOCK + 1, max(seq, MOBA_BLOCK + 2)))
    assert (far == far[0]).all(), "bias must be constant beyond the previous block"
    return own, prev, int(far[0])


def _q_kernel(x_ref, wq_ref, qg_ref, qs_ref):
    q = _dot(x_ref[...].astype(BF16), wq_ref[...]) * np.float32(HEAD_DIM ** -0.5)
    qg_ref[...] = q.astype(BF16)
    qs_ref[...] = (q * np.float32(LOG2E)).astype(BF16)


def _q_proj(x, w_q):
    s, d = x.shape
    aw = w_q.shape[1]
    tm = min(MIX_ROWS, s)
    return pl.pallas_call(
        _q_kernel,
        grid=(s // tm,),
        in_specs=[pl.BlockSpec((tm, d), lambda i: (i, 0)), _const_spec((d, aw))],
        out_specs=[pl.BlockSpec((tm, aw), lambda i: (i, 0))] * 2,
        out_shape=[jax.ShapeDtypeStruct((s, aw), BF16)] * 2,
        compiler_params=_params(),
        name="q_proj",
    )(x, w_q.astype(BF16))


def _moba_kernel(rb_ref, qg_ref, qs_ref, k_ref, vt_ref, km_ref, bo_ref, bp_ref, o_ref, bias_scr, sel_scr,
                 *, far_bucket):
    hp = pl.program_id(0)
    i = pl.program_id(1)
    tq = qg_ref.shape[0]
    nblk = km_ref.shape[0]

    @pl.when(i == 0)
    def _():
        for hh in range(HEADS_PER_STEP):
            h = hp * HEADS_PER_STEP + hh
            for kind, bkt_ref in enumerate((bo_ref, bp_ref)):
                bkt = bkt_ref[...]
                tile = jnp.zeros(bkt.shape, F32)
                for bucket in range(REL_BUCKETS):
                    tile = jnp.where(bkt == bucket, rb_ref[h, bucket], tile)
                bias_scr[hh, kind] = tile * np.float32(LOG2E)

    lane = lax.broadcasted_iota(jnp.int32, (tq, PAIR_W), 1)
    blk_row = lax.broadcasted_iota(jnp.int32, (nblk, tq), 0)
    kpos = lax.broadcasted_iota(jnp.int32, (MOBA_BLOCK, tq), 0)
    qpos = lax.broadcasted_iota(jnp.int32, (MOBA_BLOCK, tq), 1)
    j_prev = jnp.maximum(i - 1, 0)
    heads = range(HEADS_PER_STEP)

    def head_only(q_pair_ref, hh):
        mine = (lane >= hh * HEAD_DIM) & (lane < (hh + 1) * HEAD_DIM)
        return jnp.where(mine, q_pair_ref[...].astype(F32), 0.0).astype(BF16)

    qss, c_fars = [], []
    for hh in heads:
        h = hp * HEADS_PER_STEP + hh
        qss.append(head_only(qs_ref, hh))
        c_fars.append(rb_ref[h, far_bucket] * np.float32(LOG2E))
        gate = lax.dot_general(km_ref[...], head_only(qg_ref, hh), _NT, preferred_element_type=F32)
        gate = jnp.where(blk_row < i, gate, -jnp.inf)
        sel = jnp.zeros((nblk, tq), F32)
        for _ in range(MOBA_TOPK):
            top = jnp.max(gate, axis=0, keepdims=True)
            first = jnp.min(jnp.where(gate == top, blk_row, nblk), axis=0, keepdims=True)
            pick = (blk_row == first) & (top > -jnp.inf)
            sel = jnp.where(pick, 1.0, sel)
            gate = jnp.where(pick, -jnp.inf, gate)
        sel_scr[hh] = sel

    def scores(hh, k_rows):
        return lax.dot_general(k_rows, qss[hh], _NT, preferred_element_type=F32)

    def update(parts, carry):
        m, acc = carry
        m_new = m
        for x, shift, on, _ in parts:
            m_new = jnp.maximum(m_new, jnp.where(on, jnp.max(x, axis=0, keepdims=True) + shift, NEG_BIG))
        ps = [jnp.exp2(x - jnp.where(on, m_new - shift, POS_BIG)).astype(BF16) for x, shift, on, _ in parts]
        p_all = jnp.concatenate(ps, axis=0)
        vt_all = jnp.concatenate([vt for _, _, _, vt in parts], axis=1)
        return m_new, acc * jnp.exp2(m - m_new) + _dot(vt_all, p_all)

    def far_body(jj, carries):
        j0 = FAR_STEP * jj
        ks = k_ref[pl.ds(j0, FAR_STEP)].reshape(FAR_STEP * MOBA_BLOCK, PAIR_W)
        ss = [scores(hh, ks) for hh in heads]
        new = []
        for hh in heads:
            parts = []
            for t in range(FAR_STEP):
                on = (sel_scr[hh, pl.ds(j0 + t, 1), :] > 0.0) & (j0 + t < j_prev)
                parts.append((ss[hh][t * MOBA_BLOCK:(t + 1) * MOBA_BLOCK], c_fars[hh], on, vt_ref[hh, j0 + t]))
            new.append(update(parts, carries[hh]))
        return tuple(new)

    init = (jnp.full((1, tq), NEG_BIG, F32), jnp.zeros((VT_ROWS, tq), F32))
    carries = lax.fori_loop(0, (j_prev + FAR_STEP - 1) // FAR_STEP, far_body, (init,) * HEADS_PER_STEP)
    outs = []
    for hh in heads:
        on_prev = sel_scr[hh, pl.ds(j_prev, 1), :] > 0.0
        x_prev = scores(hh, k_ref[j_prev]) + bias_scr[hh, 1]
        x_own = jnp.where(qpos >= kpos, scores(hh, k_ref[i]) + bias_scr[hh, 0], NEG_BIG)
        _, acc = update([(x_prev, 0.0, on_prev, vt_ref[hh, j_prev]),
                         (x_own, 0.0, jnp.full((1, tq), True), vt_ref[hh, i])], carries[hh])
        outs.append(acc[:HEAD_DIM] / acc[HEAD_DIM:HEAD_DIM + 1])
    o_ref[...] = jnp.concatenate(outs, axis=0).T.astype(o_ref.dtype)


def _moba_attention(q_gate, q_score, k4, vt4, km, rel_bias):
    s, aw = q_gate.shape
    nblk = s // MOBA_BLOCK
    npair = aw // PAIR_W
    own, prev, far_bucket = _bucket_tiles(s)
    assert nblk % FAR_STEP == 0
    tile = pl.BlockSpec((MOBA_BLOCK, PAIR_W), lambda hp, i: (i, hp))
    return pl.pallas_call(
        functools.partial(_moba_kernel, far_bucket=far_bucket),
        grid=(npair, nblk),
        in_specs=[pl.BlockSpec(memory_space=pltpu.SMEM),
                  tile, tile,
                  pl.BlockSpec((None, nblk, MOBA_BLOCK, PAIR_W), lambda hp, i: (hp, 0, 0, 0)),
                  pl.BlockSpec((HEADS_PER_STEP, nblk, VT_ROWS, MOBA_BLOCK), lambda hp, i: (hp, 0, 0, 0)),
                  pl.BlockSpec((nblk, PAIR_W), lambda hp, i: (0, hp)),
                  pl.BlockSpec((MOBA_BLOCK, MOBA_BLOCK), lambda hp, i: (0, 0)),
                  pl.BlockSpec((MOBA_BLOCK, MOBA_BLOCK), lambda hp, i: (0, 0))],
        out_specs=tile,
        out_shape=jax.ShapeDtypeStruct((s, aw), BF16),
        scratch_shapes=[pltpu.VMEM((HEADS_PER_STEP, 2, MOBA_BLOCK, MOBA_BLOCK), F32),
                        pltpu.VMEM((HEADS_PER_STEP, nblk, MOBA_BLOCK), F32)],
        compiler_params=_params(2),
        name="moba_attn",
    )(rel_bias, q_gate, q_score, k4, vt4, km, jnp.asarray(own), jnp.asarray(prev))


def _proj_kernel(a_ref, x_ref, w_ref, g_ref, b_ref, o_ref):
    y = DEEPNORM_ALPHA * x_ref[...] + _dot(a_ref[...], w_ref[...])
    o_ref[...] = _layer_norm(y, g_ref[...], b_ref[...])


def _proj_ln(a, x, w, g, b):
    s, d = x.shape
    aw = a.shape[1]
    tm = min(MIX_ROWS, s)
    return pl.pallas_call(
        _proj_kernel,
        grid=(s // tm,),
        in_specs=[pl.BlockSpec((tm, aw), lambda i: (i, 0)), pl.BlockSpec((tm, d), lambda i: (i, 0)),
                  _const_spec((aw, d)), _const_spec((1, d)), _const_spec((1, d))],
        out_specs=pl.BlockSpec((tm, d), lambda i: (i, 0)),
        out_shape=jax.ShapeDtypeStruct((s, d), F32),
        compiler_params=_params(),
        name="attn_out_ln",
    )(a, x, w.astype(BF16), g.reshape(1, d), b.reshape(1, d))


def kernel(x, ln_g, ln_b, ffn_w_gate, ffn_w_up, ffn_w_down, gm_w_in, gm_sgu_g, gm_sgu_b, gm_w_s, gm_b_s, gm_w_out, attn_w_q, attn_w_o, w_k_shared, w_v_shared, rel_bias):
    batch, seq, d = x.shape
    assert seq % MOBA_BLOCK == 0 and w_k_shared.shape[1] == N_HEADS * HEAD_DIM
    outs = []
    for bi in range(batch):
        h = x[bi]
        kv = None
        for layer in range(DEPTH):
            h = _ffn_ln(h, ffn_w_gate[layer, 0], ffn_w_up[layer, 0], ffn_w_down[layer, 0],
                        ln_g[layer, 0], ln_b[layer, 0])
            if layer < N_A_LAYERS:
                a = layer
                h = _gmlp_ln(h, gm_w_in[a], gm_sgu_g[a], gm_sgu_b[a], gm_w_s[a], gm_b_s[a], gm_w_out[a],
                             ln_g[layer, 1], ln_b[layer, 1])
            else:
                j = layer - N_A_LAYERS
                o = _moba_attention(*_q_proj(h, attn_w_q[j]), *kv, rel_bias)
                h = _proj_ln(o, h, attn_w_o[j], ln_g[layer, 1], ln_b[layer, 1])
            h = _ffn_ln(h, ffn_w_gate[layer, 1], ffn_w_up[layer, 1], ffn_w_down[layer, 1],
                        ln_g[layer, 2], ln_b[layer, 2])
            if layer == N_A_LAYERS - 1:
                kv = _shared_kv(h, w_k_shared, w_v_shared)
        outs.append(h)
    return jnp.stack(outs)
```

```python
import functools
import math

import jax
import jax.numpy as jnp
import numpy as np
from jax import lax
from jax.experimental import pallas as pl
from jax.experimental.pallas import tpu as pltpu

DEPTH = 4
N_A_LAYERS = DEPTH // 2
GM_GROUPS = 8
GM_CHUNK = 128
N_HEADS = 16
HEAD_DIM = 64
MOBA_BLOCK = 256
MOBA_TOPK = 3
REL_BUCKETS = 32
REL_MAX_DIST = 128
DEEPNORM_ALPHA = (2 * DEPTH) ** 0.25
LN_EPS = 1e-5

HEADS_PER_STEP = 2
PAIR_W = HEADS_PER_STEP * HEAD_DIM
VT_ROWS = HEAD_DIM + 16
FAR_STEP = 4
LOG2E = math.log2(math.e)
NEG_BIG = -1e30
POS_BIG = 1e30

FFN_ROWS = 512
FFN_CHUNK = 256
MIX_ROWS = 512
VMEM_LIMIT = 56 * 1024 * 1024

F32 = jnp.float32
BF16 = jnp.bfloat16
_NT = (((1,), (1,)), ((), ()))


def _params(n_axes=1):
    return pltpu.CompilerParams(dimension_semantics=("arbitrary",) * n_axes,
                                vmem_limit_bytes=VMEM_LIMIT)


def _const_spec(shape):
    zeros = (0,) * len(shape)
    return pl.BlockSpec(shape, lambda *_: zeros, pipeline_mode=pl.Buffered(1))


def _layer_norm(y, g, b):
    mu = jnp.mean(y, axis=-1, keepdims=True)
    d = y - mu
    var = jnp.mean(d * d, axis=-1, keepdims=True)
    return d * lax.rsqrt(var + LN_EPS) * g + b


def _dot(a, b):
    return jnp.dot(a, b, preferred_element_type=F32)


def _ffn_kernel(x_ref, wg_ref, wu_ref, wd_ref, g_ref, b_ref, o_ref):
    x = x_ref[...]
    xb = x.astype(BF16)
    h1 = _dot(xb, wg_ref[...])
    h2 = _dot(xb, wu_ref[...])
    a = ((h1 * jax.nn.sigmoid(h1)) * h2).astype(BF16)
    y = DEEPNORM_ALPHA * x + 0.5 * _dot(a, wd_ref[...])
    o_ref[...] = _layer_norm(y, g_ref[...], b_ref[...])


def _ffn_ln(x, w_gate, w_up, w_down, g, b):
    s, d = x.shape
    f = w_gate.shape[1]
    tm = min(FFN_ROWS, s)
    row = pl.BlockSpec((tm, d), lambda i: (i, 0))
    return pl.pallas_call(
        _ffn_kernel,
        grid=(s // tm,),
        in_specs=[row, _const_spec((d, f)), _const_spec((d, f)), _const_spec((f, d)),
                  _const_spec((1, d)), _const_spec((1, d))],
        out_specs=row,
        out_shape=jax.ShapeDtypeStruct((s, d), F32),
        compiler_params=_params(),
        name="ffn_ln",
    )(x, w_gate.astype(BF16), w_up.astype(BF16), w_down.astype(BF16), g.reshape(1, d), b.reshape(1, d))


def _gmlp_kernel(x_ref, win_ref, sg_ref, sb_ref, ws_ref, bs_ref, wout_ref, g_ref, b_ref, o_ref,
                 v_scr, mix_scr):
    tm = x_ref.shape[0]
    w = wout_ref.shape[0]
    gd = w // GM_GROUPS
    nch = tm // GM_CHUNK
    x = x_ref[...]
    z = _dot(x.astype(BF16), win_ref[...])
    z = 0.5 * z * (1.0 + lax.erf(z * np.float32(math.sqrt(0.5))))
    u = z[:, :w]
    v_scr[...] = _layer_norm(z[:, w:], sg_ref[...], sb_ref[...]).astype(BF16)
    t_idx = lax.broadcasted_iota(jnp.int32, (GM_CHUNK, GM_CHUNK), 0)
    s_idx = lax.broadcasted_iota(jnp.int32, (GM_CHUNK, GM_CHUNK), 1)
    for grp in range(GM_GROUPS):
        cols = slice(grp * gd, (grp + 1) * gd)
        w_tril = jnp.where(t_idx >= s_idx, ws_ref[grp], 0.0).astype(BF16)
        vg = jnp.concatenate([v_scr[c * GM_CHUNK:(c + 1) * GM_CHUNK, cols] for c in range(nch)], axis=1)
        mg = _dot(w_tril, vg)
        for c in range(nch):
            mix_scr[c * GM_CHUNK:(c + 1) * GM_CHUNK, cols] = mg[:, c * gd:(c + 1) * gd] + bs_ref[:, cols]
    y = (u * mix_scr[...]).astype(BF16)
    out = DEEPNORM_ALPHA * x + _dot(y, wout_ref[...])
    o_ref[...] = _layer_norm(out, g_ref[...], b_ref[...])


def _gmlp_ln(x, w_in, sgu_g, sgu_b, w_s, b_s, w_out, g, b):
    s, d = x.shape
    w = w_out.shape[0]
    tm = min(MIX_ROWS, s)
    bs_full = jnp.repeat(b_s.T, w // GM_GROUPS, axis=1)
    row = pl.BlockSpec((tm, d), lambda i: (i, 0))
    return pl.pallas_call(
        _gmlp_kernel,
        grid=(s // tm,),
        in_specs=[row, _const_spec((d, 2 * w)), _const_spec((1, w)), _const_spec((1, w)),
                  _const_spec((GM_GROUPS, GM_CHUNK, GM_CHUNK)), _const_spec((GM_CHUNK, w)),
                  _const_spec((w, d)), _const_spec((1, d)), _const_spec((1, d))],
        out_specs=row,
        out_shape=jax.ShapeDtypeStruct((s, d), F32),
        scratch_shapes=[pltpu.VMEM((tm, w), BF16), pltpu.VMEM((tm, w), F32)],
        compiler_params=_params(),
        name="gmlp_ln",
    )(x, w_in.astype(BF16), sgu_g.reshape(1, w), sgu_b.reshape(1, w), w_s, bs_full,
      w_out.astype(BF16), g.reshape(1, d), b.reshape(1, d))


def _kv_kernel(x_ref, wk_ref, wvt_ref, k_ref, vt_ref, km_ref):
    tm = x_ref.shape[0]
    xb = x_ref[...].astype(BF16)
    k = _dot(xb, wk_ref[...])
    vt = lax.dot_general(wvt_ref[...], xb, _NT, preferred_element_type=F32)
    kb = k.astype(BF16)
    vtb = vt.astype(BF16)
    pad_row = lax.broadcasted_iota(jnp.int32, (VT_ROWS - HEAD_DIM, MOBA_BLOCK), 0)
    pad = jnp.where(pad_row == 0, 1.0, 0.0).astype(BF16)
    for blk in range(tm // MOBA_BLOCK):
        rows = slice(blk * MOBA_BLOCK, (blk + 1) * MOBA_BLOCK)
        km_ref[blk] = jnp.mean(k[rows], axis=0, keepdims=True)
        for p in range(k_ref.shape[0]):
            k_ref[p, blk] = kb[rows, p * PAIR_W:(p + 1) * PAIR_W]
        for h in range(vt_ref.shape[0]):
            vt_ref[h, blk] = jnp.concatenate([vtb[h * HEAD_DIM:(h + 1) * HEAD_DIM, rows], pad], axis=0)


def _shared_kv(x, w_k, w_v):
    s, d = x.shape
    aw = w_k.shape[1]
    nblk = s // MOBA_BLOCK
    tm = min(2 * MOBA_BLOCK, s)
    r = tm // MOBA_BLOCK
    npair = aw // PAIR_W
    k4, vt4, km = pl.pallas_call(
        _kv_kernel,
        grid=(s // tm,),
        in_specs=[pl.BlockSpec((tm, d), lambda i: (i, 0)), _const_spec((d, aw)), _const_spec((aw, d))],
        out_specs=[pl.BlockSpec((npair, r, MOBA_BLOCK, PAIR_W), lambda i: (0, i, 0, 0)),
                   pl.BlockSpec((N_HEADS, r, VT_ROWS, MOBA_BLOCK), lambda i: (0, i, 0, 0)),
                   pl.BlockSpec((r, 1, aw), lambda i: (i, 0, 0))],
        out_shape=[jax.ShapeDtypeStruct((npair, nblk, MOBA_BLOCK, PAIR_W), BF16),
                   jax.ShapeDtypeStruct((N_HEADS, nblk, VT_ROWS, MOBA_BLOCK), BF16),
                   jax.ShapeDtypeStruct((nblk, 1, aw), F32)],
        compiler_params=_params(),
        name="shared_kv",
    )(x, w_k.astype(BF16), w_v.T.astype(BF16))
    return k4, vt4, km.reshape(nblk, aw).astype(BF16)


def _t5_bucket_np(dist):
    n = np.maximum(dist, 0)
    max_exact = REL_BUCKETS // 2
    nf = np.maximum(n, 1).astype(np.float32)
    large = max_exact + (np.log(nf / np.float32(max_exact)) / np.float32(math.log(REL_MAX_DIST / max_exact))
                         * np.float32(REL_BUCKETS - max_exact)).astype(np.int32)
    large = np.minimum(large, REL_BUCKETS - 1)
    return np.where(n < max_exact, n, large).astype(np.int32)


def _bucket_tiles(seq):
    kpos = np.arange(MOBA_BLOCK)[:, None]
    qpos = np.arange(MOBA_BLOCK)[None, :]
    own = _t5_bucket_np(qpos - kpos)
    prev = _t5_bucket_np(qpos - kpos + MOBA_BLOCK)
    far = _t5_bucket_np(np.arange(MOBA_BLOCK + 1, max(seq, MOBA_BLOCK + 2)))
    assert (far == far[0]).all(), "bias must be constant beyond the previous block"
    return own, prev, int(far[0])


def _q_kernel(x_ref, wq_ref, qg_ref, qs_ref):
    q = _dot(x_ref[...].astype(BF16), wq_ref[...]) * np.float32(HEAD_DIM ** -0.5)
    qg_ref[...] = q.astype(BF16)
    qs_ref[...] = (q * np.float32(LOG2E)).astype(BF16)


def _q_proj(x, w_q):
    s, d = x.shape
    aw = w_q.shape[1]
    tm = min(MIX_ROWS, s)
    return pl.pallas_call(
        _q_kernel,
        grid=(s // tm,),
        in_specs=[pl.BlockSpec((tm, d), lambda i: (i, 0)), _const_spec((d, aw))],
        out_specs=[pl.BlockSpec((tm, aw), lambda i: (i, 0))] * 2,
        out_shape=[jax.ShapeDtypeStruct((s, aw), BF16)] * 2,
        compiler_params=_params(),
        name="q_proj",
    )(x, w_q.astype(BF16))


def _moba_kernel(rb_ref, qg_ref, qs_ref, k_ref, vt_ref, km_ref, bo_ref, bp_ref, o_ref, bias_scr, sel_scr,
                 s_scr, *, far_bucket):
    hp = pl.program_id(0)
    i = pl.program_id(1)
    tq = qg_ref.shape[0]
    nblk = km_ref.shape[0]

    @pl.when(i == 0)
    def _():
        for hh in range(HEADS_PER_STEP):
            h = hp * HEADS_PER_STEP + hh
            for kind, bkt_ref in enumerate((bo_ref, bp_ref)):
                bkt = bkt_ref[...]
                tile = jnp.zeros(bkt.shape, F32)
                for bucket in range(REL_BUCKETS):
                    tile = jnp.where(bkt == bucket, rb_ref[h, bucket], tile)
                bias_scr[hh, kind] = tile * np.float32(LOG2E)

    lane = lax.broadcasted_iota(jnp.int32, (tq, PAIR_W), 1)
    blk_row = lax.broadcasted_iota(jnp.int32, (nblk, tq), 0)
    kpos = lax.broadcasted_iota(jnp.int32, (MOBA_BLOCK, tq), 0)
    qpos = lax.broadcasted_iota(jnp.int32, (MOBA_BLOCK, tq), 1)
    j_prev = jnp.maximum(i - 1, 0)
    heads = range(HEADS_PER_STEP)

    def head_only(q_pair_ref, hh):
        mine = (lane >= hh * HEAD_DIM) & (lane < (hh + 1) * HEAD_DIM)
        return jnp.where(mine, q_pair_ref[...].astype(F32), 0.0).astype(BF16)

    qss, c_fars = [], []
    for hh in heads:
        h = hp * HEADS_PER_STEP + hh
        qss.append(head_only(qs_ref, hh))
        c_fars.append(rb_ref[h, far_bucket] * np.float32(LOG2E))
        gate = lax.dot_general(km_ref[...], head_only(qg_ref, hh), _NT, preferred_element_type=F32)
        gate = jnp.where(blk_row < i, gate, -jnp.inf)
        sel = jnp.zeros((nblk, tq), F32)
        for _ in range(MOBA_TOPK):
            top = jnp.max(gate, axis=0, keepdims=True)
            first = jnp.min(jnp.where(gate == top, blk_row, nblk), axis=0, keepdims=True)
            pick = (blk_row == first) & (top > -jnp.inf)
            sel = jnp.where(pick, 1.0, sel)
            gate = jnp.where(pick, -jnp.inf, gate)
        sel_scr[hh] = sel

    def scores(hh, k_rows):
        return lax.dot_general(k_rows, qss[hh], _NT, preferred_element_type=F32)

    def update(parts, carry):
        m, acc = carry
        m_new = m
        for x, shift, on, _ in parts:
            m_new = jnp.maximum(m_new, jnp.where(on, jnp.max(x, axis=0, keepdims=True) + shift, NEG_BIG))
        ps = [jnp.exp2(x - jnp.where(on, m_new - shift, POS_BIG)).astype(BF16) for x, shift, on, _ in parts]
        p_all = jnp.concatenate(ps, axis=0)
        vt_all = jnp.concatenate([vt for _, _, _, vt in parts], axis=1)
        return m_new, acc * jnp.exp2(m - m_new) + _dot(vt_all, p_all)

    def far_scores(step, slot):
        j0 = jnp.minimum(FAR_STEP * step, nblk - FAR_STEP)
        ks = k_ref[pl.ds(j0, FAR_STEP)].reshape(FAR_STEP * MOBA_BLOCK, PAIR_W)
        for hh in heads:
            s_scr[slot, hh] = scores(hh, ks)

    def far_consume(step, slot, carries):
        new = []
        for hh in heads:
            parts = []
            for t in range(FAR_STEP):
                j = FAR_STEP * step + t
                jc = jnp.minimum(j, nblk - 1)
                on = (sel_scr[hh, pl.ds(jc, 1), :] > 0.0) & (j < j_prev)
                parts.append((s_scr[slot, hh, t * MOBA_BLOCK:(t + 1) * MOBA_BLOCK], c_fars[hh], on,
                              vt_ref[hh, jc]))
            new.append(update(parts, carries[hh]))
        return tuple(new)

    def far_trip(t, carries):
        far_scores(2 * t + 1, 1)
        carries = far_consume(2 * t, 0, carries)
        far_scores(2 * t + 2, 0)
        return far_consume(2 * t + 1, 1, carries)

    n_steps = (j_prev + FAR_STEP - 1) // FAR_STEP
    far_scores(0, 0)
    near = [(scores(hh, k_ref[j_prev]), scores(hh, k_ref[i])) for hh in heads]
    init = (jnp.full((1, tq), NEG_BIG, F32), jnp.zeros((VT_ROWS, tq), F32))
    carries = []
    for hh in heads:
        on_prev = sel_scr[hh, pl.ds(j_prev, 1), :] > 0.0
        x_prev = near[hh][0] + bias_scr[hh, 1]
        x_own = jnp.where(qpos >= kpos, near[hh][1] + bias_scr[hh, 0], NEG_BIG)
        carries.append(update([(x_prev, 0.0, on_prev, vt_ref[hh, j_prev]),
                               (x_own, 0.0, jnp.full((1, tq), True), vt_ref[hh, i])], init))
    carries = lax.fori_loop(0, (n_steps + 1) // 2, far_trip, tuple(carries))
    outs = [acc[:HEAD_DIM] / acc[HEAD_DIM:HEAD_DIM + 1] for _, acc in carries]
    o_ref[...] = jnp.concatenate(outs, axis=0).T.astype(o_ref.dtype)


def _moba_attention(q_gate, q_score, k4, vt4, km, rel_bias):
    s, aw = q_gate.shape
    nblk = s // MOBA_BLOCK
    npair = aw // PAIR_W
    own, prev, far_bucket = _bucket_tiles(s)
    assert nblk % FAR_STEP == 0
    tile = pl.BlockSpec((MOBA_BLOCK, PAIR_W), lambda hp, i: (i, hp))
    return pl.pallas_call(
        functools.partial(_moba_kernel, far_bucket=far_bucket),
        grid=(npair, nblk),
        in_specs=[pl.BlockSpec(memory_space=pltpu.SMEM),
                  tile, tile,
                  pl.BlockSpec((None, nblk, MOBA_BLOCK, PAIR_W), lambda hp, i: (hp, 0, 0, 0)),
                  pl.BlockSpec((HEADS_PER_STEP, nblk, VT_ROWS, MOBA_BLOCK), lambda hp, i: (hp, 0, 0, 0)),
                  pl.BlockSpec((nblk, PAIR_W), lambda hp, i: (0, hp)),
                  pl.BlockSpec((MOBA_BLOCK, MOBA_BLOCK), lambda hp, i: (0, 0)),
                  pl.BlockSpec((MOBA_BLOCK, MOBA_BLOCK), lambda hp, i: (0, 0))],
        out_specs=tile,
        out_shape=jax.ShapeDtypeStruct((s, aw), BF16),
        scratch_shapes=[pltpu.VMEM((HEADS_PER_STEP, 2, MOBA_BLOCK, MOBA_BLOCK), F32),
                        pltpu.VMEM((HEADS_PER_STEP, nblk, MOBA_BLOCK), F32),
                        pltpu.VMEM((2, HEADS_PER_STEP, FAR_STEP * MOBA_BLOCK, MOBA_BLOCK), F32)],
        compiler_params=_params(2),
        name="moba_attn",
    )(rel_bias, q_gate, q_score, k4, vt4, km, jnp.asarray(own), jnp.asarray(prev))


def _proj_kernel(a_ref, x_ref, w_ref, g_ref, b_ref, o_ref):
    y = DEEPNORM_ALPHA * x_ref[...] + _dot(a_ref[...], w_ref[...])
    o_ref[...] = _layer_norm(y, g_ref[...], b_ref[...])


def _proj_ln(a, x, w, g, b):
    s, d = x.shape
    aw = a.shape[1]
    tm = min(MIX_ROWS, s)
    return pl.pallas_call(
        _proj_kernel,
        grid=(s // tm,),
        in_specs=[pl.BlockSpec((tm, aw), lambda i: (i, 0)), pl.BlockSpec((tm, d), lambda i: (i, 0)),
                  _const_spec((aw, d)), _const_spec((1, d)), _const_spec((1, d))],
        out_specs=pl.BlockSpec((tm, d), lambda i: (i, 0)),
        out_shape=jax.ShapeDtypeStruct((s, d), F32),
        compiler_params=_params(),
        name="attn_out_ln",
    )(a, x, w.astype(BF16), g.reshape(1, d), b.reshape(1, d))


def kernel(x, ln_g, ln_b, ffn_w_gate, ffn_w_up, ffn_w_down, gm_w_in, gm_sgu_g, gm_sgu_b, gm_w_s, gm_b_s, gm_w_out, attn_w_q, attn_w_o, w_k_shared, w_v_shared, rel_bias):
    batch, seq, d = x.shape
    assert seq % MOBA_BLOCK == 0 and w_k_shared.shape[1] == N_HEADS * HEAD_DIM
    outs = []
    for bi in range(batch):
        h = x[bi]
        kv = None
        for layer in range(DEPTH):
            h = _ffn_ln(h, ffn_w_gate[layer, 0], ffn_w_up[layer, 0], ffn_w_down[layer, 0],
                        ln_g[layer, 0], ln_b[layer, 0])
            if layer < N_A_LAYERS:
                a = layer
                h = _gmlp_ln(h, gm_w_in[a], gm_sgu_g[a], gm_sgu_b[a], gm_w_s[a], gm_b_s[a], gm_w_out[a],
                             ln_g[layer, 1], ln_b[layer, 1])
            else:
                j = layer - N_A_LAYERS
                o = _moba_attention(*_q_proj(h, attn_w_q[j]), *kv, rel_bias)
                h = _proj_ln(o, h, attn_w_o[j], ln_g[layer, 1], ln_b[layer, 1])
            h = _ffn_ln(h, ffn_w_gate[layer, 1], ffn_w_up[layer, 1], ffn_w_down[layer, 1],
                        ln_g[layer, 2], ln_b[layer, 2])
            if layer == N_A_LAYERS - 1:
                kv = _shared_kv(h, w_k_shared, w_v_shared)
        outs.append(h)
    return jnp.stack(outs)
```

```python
import functools
import math

import jax
import jax.numpy as jnp
import numpy as np
from jax import lax
from jax.experimental import pallas as pl
from jax.experimental.pallas import tpu as pltpu

DEPTH = 4
N_A_LAYERS = DEPTH // 2
GM_GROUPS = 8
GM_CHUNK = 128
N_HEADS = 16
HEAD_DIM = 64
MOBA_BLOCK = 256
MOBA_TOPK = 3
REL_BUCKETS = 32
REL_MAX_DIST = 128
DEEPNORM_ALPHA = (2 * DEPTH) ** 0.25
LN_EPS = 1e-5

HEADS_PER_STEP = 2
PAIR_W = HEADS_PER_STEP * HEAD_DIM
VT_ROWS = HEAD_DIM + 16
STEP_BLOCKS = 4
LONG_TRIP_STEPS = 4
KIND_OWN, KIND_PREV, KIND_FAR = 0, 1, 2
LOG2E = math.log2(math.e)
NEG_BIG = -1e30
POS_BIG = 1e30

FFN_ROWS = 512
FFN_CHUNK = 256
MIX_ROWS = 512
VMEM_LIMIT = 56 * 1024 * 1024

F32 = jnp.float32
BF16 = jnp.bfloat16
_NT = (((1,), (1,)), ((), ()))


def _params(n_axes=1):
    return pltpu.CompilerParams(dimension_semantics=("arbitrary",) * n_axes,
                                vmem_limit_bytes=VMEM_LIMIT)


def _const_spec(shape):
    zeros = (0,) * len(shape)
    return pl.BlockSpec(shape, lambda *_: zeros, pipeline_mode=pl.Buffered(1))


def _layer_norm(y, g, b):
    mu = jnp.mean(y, axis=-1, keepdims=True)
    d = y - mu
    var = jnp.mean(d * d, axis=-1, keepdims=True)
    return d * lax.rsqrt(var + LN_EPS) * g + b


def _dot(a, b):
    return jnp.dot(a, b, preferred_element_type=F32)


def _ffn_kernel(x_ref, wg_ref, wu_ref, wd_ref, g_ref, b_ref, o_ref):
    x = x_ref[...]
    xb = x.astype(BF16)
    h1 = _dot(xb, wg_ref[...])
    h2 = _dot(xb, wu_ref[...])
    a = ((h1 * jax.nn.sigmoid(h1)) * h2).astype(BF16)
    y = DEEPNORM_ALPHA * x + 0.5 * _dot(a, wd_ref[...])
    o_ref[...] = _layer_norm(y, g_ref[...], b_ref[...])


def _ffn_ln(x, w_gate, w_up, w_down, g, b):
    s, d = x.shape
    f = w_gate.shape[1]
    tm = min(FFN_ROWS, s)
    row = pl.BlockSpec((tm, d), lambda i: (i, 0))
    return pl.pallas_call(
        _ffn_kernel,
        grid=(s // tm,),
        in_specs=[row, _const_spec((d, f)), _const_spec((d, f)), _const_spec((f, d)),
                  _const_spec((1, d)), _const_spec((1, d))],
        out_specs=row,
        out_shape=jax.ShapeDtypeStruct((s, d), F32),
        compiler_params=_params(),
        name="ffn_ln",
    )(x, w_gate.astype(BF16), w_up.astype(BF16), w_down.astype(BF16), g.reshape(1, d), b.reshape(1, d))


def _gmlp_kernel(x_ref, win_ref, sg_ref, sb_ref, ws_ref, bs_ref, wout_ref, g_ref, b_ref, o_ref,
                 v_scr, mix_scr):
    tm = x_ref.shape[0]
    w = wout_ref.shape[0]
    gd = w // GM_GROUPS
    nch = tm // GM_CHUNK
    x = x_ref[...]
    z = _dot(x.astype(BF16), win_ref[...])
    z = 0.5 * z * (1.0 + lax.erf(z * np.float32(math.sqrt(0.5))))
    u = z[:, :w]
    v_scr[...] = _layer_norm(z[:, w:], sg_ref[...], sb_ref[...]).astype(BF16)
    t_idx = lax.broadcasted_iota(jnp.int32, (GM_CHUNK, GM_CHUNK), 0)
    s_idx = lax.broadcasted_iota(jnp.int32, (GM_CHUNK, GM_CHUNK), 1)
    for grp in range(GM_GROUPS):
        cols = slice(grp * gd, (grp + 1) * gd)
        w_tril = jnp.where(t_idx >= s_idx, ws_ref[grp], 0.0).astype(BF16)
        vg = jnp.concatenate([v_scr[c * GM_CHUNK:(c + 1) * GM_CHUNK, cols] for c in range(nch)], axis=1)
        mg = _dot(w_tril, vg)
        for c in range(nch):
            mix_scr[c * GM_CHUNK:(c + 1) * GM_CHUNK, cols] = mg[:, c * gd:(c + 1) * gd] + bs_ref[:, cols]
    y = (u * mix_scr[...]).astype(BF16)
    out = DEEPNORM_ALPHA * x + _dot(y, wout_ref[...])
    o_ref[...] = _layer_norm(out, g_ref[...], b_ref[...])


def _gmlp_ln(x, w_in, sgu_g, sgu_b, w_s, b_s, w_out, g, b):
    s, d = x.shape
    w = w_out.shape[0]
    tm = min(MIX_ROWS, s)
    bs_full = jnp.repeat(b_s.T, w // GM_GROUPS, axis=1)
    row = pl.BlockSpec((tm, d), lambda i: (i, 0))
    return pl.pallas_call(
        _gmlp_kernel,
        grid=(s // tm,),
        in_specs=[row, _const_spec((d, 2 * w)), _const_spec((1, w)), _const_spec((1, w)),
                  _const_spec((GM_GROUPS, GM_CHUNK, GM_CHUNK)), _const_spec((GM_CHUNK, w)),
                  _const_spec((w, d)), _const_spec((1, d)), _const_spec((1, d))],
        out_specs=row,
        out_shape=jax.ShapeDtypeStruct((s, d), F32),
        scratch_shapes=[pltpu.VMEM((tm, w), BF16), pltpu.VMEM((tm, w), F32)],
        compiler_params=_params(),
        name="gmlp_ln",
    )(x, w_in.astype(BF16), sgu_g.reshape(1, w), sgu_b.reshape(1, w), w_s, bs_full,
      w_out.astype(BF16), g.reshape(1, d), b.reshape(1, d))


def _kv_kernel(x_ref, wk_ref, wvt_ref, k_ref, vt_ref, km_ref):
    tm = x_ref.shape[0]
    xb = x_ref[...].astype(BF16)
    k = _dot(xb, wk_ref[...])
    vt = lax.dot_general(wvt_ref[...], xb, _NT, preferred_element_type=F32)
    kb = k.astype(BF16)
    vtb = vt.astype(BF16)
    pad_row = lax.broadcasted_iota(jnp.int32, (VT_ROWS - HEAD_DIM, MOBA_BLOCK), 0)
    pad = jnp.where(pad_row == 0, 1.0, 0.0).astype(BF16)
    for blk in range(tm // MOBA_BLOCK):
        rows = slice(blk * MOBA_BLOCK, (blk + 1) * MOBA_BLOCK)
        km_ref[blk] = jnp.mean(k[rows], axis=0, keepdims=True)
        for p in range(k_ref.shape[0]):
            k_ref[p, blk] = kb[rows, p * PAIR_W:(p + 1) * PAIR_W]
        for h in range(vt_ref.shape[0]):
            vt_ref[h, blk] = jnp.concatenate([vtb[h * HEAD_DIM:(h + 1) * HEAD_DIM, rows], pad], axis=0)


def _shared_kv(x, w_k, w_v):
    s, d = x.shape
    aw = w_k.shape[1]
    nblk = s // MOBA_BLOCK
    tm = min(2 * MOBA_BLOCK, s)
    r = tm // MOBA_BLOCK
    npair = aw // PAIR_W
    k4, vt4, km = pl.pallas_call(
        _kv_kernel,
        grid=(s // tm,),
        in_specs=[pl.BlockSpec((tm, d), lambda i: (i, 0)), _const_spec((d, aw)), _const_spec((aw, d))],
        out_specs=[pl.BlockSpec((npair, r, MOBA_BLOCK, PAIR_W), lambda i: (0, i, 0, 0)),
                   pl.BlockSpec((N_HEADS, r, VT_ROWS, MOBA_BLOCK), lambda i: (0, i, 0, 0)),
                   pl.BlockSpec((r, 1, aw), lambda i: (i, 0, 0))],
        out_shape=[jax.ShapeDtypeStruct((npair, nblk, MOBA_BLOCK, PAIR_W), BF16),
                   jax.ShapeDtypeStruct((N_HEADS, nblk, VT_ROWS, MOBA_BLOCK), BF16),
                   jax.ShapeDtypeStruct((nblk, 1, aw), F32)],
        compiler_params=_params(),
        name="shared_kv",
    )(x, w_k.astype(BF16), w_v.T.astype(BF16))
    return k4, vt4, km.reshape(nblk, aw).astype(BF16)


def _t5_bucket_np(dist):
    n = np.maximum(dist, 0)
    max_exact = REL_BUCKETS // 2
    nf = np.maximum(n, 1).astype(np.float32)
    large = max_exact + (np.log(nf / np.float32(max_exact)) / np.float32(math.log(REL_MAX_DIST / max_exact))
                         * np.float32(REL_BUCKETS - max_exact)).astype(np.int32)
    large = np.minimum(large, REL_BUCKETS - 1)
    return np.where(n < max_exact, n, large).astype(np.int32)


def _bucket_tiles(seq):
    kpos = np.arange(MOBA_BLOCK)[:, None]
    qpos = np.arange(MOBA_BLOCK)[None, :]
    own = _t5_bucket_np(qpos - kpos)
    prev = _t5_bucket_np(qpos - kpos + MOBA_BLOCK)
    far = _t5_bucket_np(np.arange(MOBA_BLOCK + 1, max(seq, MOBA_BLOCK + 2)))
    assert (far == far[0]).all(), "bias must be constant beyond the previous block"
    return own, prev, int(far[0])


def _q_kernel(x_ref, wq_ref, qg_ref, qs_ref):
    q = _dot(x_ref[...].astype(BF16), wq_ref[...]) * np.float32(HEAD_DIM ** -0.5)
    qg_ref[...] = q.astype(BF16)
    qs_ref[...] = (q * np.float32(LOG2E)).astype(BF16)


def _q_proj(x, w_q):
    s, d = x.shape
    aw = w_q.shape[1]
    tm = min(MIX_ROWS, s)
    return pl.pallas_call(
        _q_kernel,
        grid=(s // tm,),
        in_specs=[pl.BlockSpec((tm, d), lambda i: (i, 0)), _const_spec((d, aw))],
        out_specs=[pl.BlockSpec((tm, aw), lambda i: (i, 0))] * 2,
        out_shape=[jax.ShapeDtypeStruct((s, aw), BF16)] * 2,
        compiler_params=_params(),
        name="q_proj",
    )(x, w_q.astype(BF16))


def _moba_kernel(rb_ref, qg_ref, qs_ref, k_ref, vt_ref, km_ref, bo_ref, bp_ref, o_ref, bias_scr, sel_scr,
                 s_scr, *, far_bucket):
    hp = pl.program_id(0)
    i = pl.program_id(1)
    tq = qg_ref.shape[0]
    nblk = km_ref.shape[0]
    heads = range(HEADS_PER_STEP)

    @pl.when(i == 0)
    def _():
        kpos = lax.broadcasted_iota(jnp.int32, (MOBA_BLOCK, tq), 0)
        qpos = lax.broadcasted_iota(jnp.int32, (MOBA_BLOCK, tq), 1)
        for hh in heads:
            h = hp * HEADS_PER_STEP + hh
            for kind, bkt_ref in ((KIND_OWN, bo_ref), (KIND_PREV, bp_ref)):
                bkt = bkt_ref[...]
                tile = jnp.zeros(bkt.shape, F32)
                for bucket in range(REL_BUCKETS):
                    tile = jnp.where(bkt == bucket, rb_ref[h, bucket], tile)
                tile = tile * np.float32(LOG2E)
                if kind == KIND_OWN:
                    tile = jnp.where(qpos >= kpos, tile, NEG_BIG)
                bias_scr[hh, kind] = tile
            bias_scr[hh, KIND_FAR] = jnp.full((MOBA_BLOCK, tq), rb_ref[h, far_bucket] * np.float32(LOG2E), F32)

    lane = lax.broadcasted_iota(jnp.int32, (tq, PAIR_W), 1)
    blk_row = lax.broadcasted_iota(jnp.int32, (nblk, tq), 0)

    def head_only(q_pair_ref, hh):
        mine = (lane >= hh * HEAD_DIM) & (lane < (hh + 1) * HEAD_DIM)
        return jnp.where(mine, q_pair_ref[...].astype(F32), 0.0).astype(BF16)

    qss = [head_only(qs_ref, hh) for hh in heads]

    def scores(step, slot):
        j0 = jnp.minimum(STEP_BLOCKS * step, nblk - STEP_BLOCKS)
        ks = k_ref[pl.ds(j0, STEP_BLOCKS)].reshape(STEP_BLOCKS * MOBA_BLOCK, PAIR_W)
        for hh in heads:
            s_scr[slot, hh] = lax.dot_general(ks, qss[hh], _NT, preferred_element_type=F32)

    gates = [lax.dot_general(km_ref[...], head_only(qg_ref, hh), _NT, preferred_element_type=F32)
             for hh in heads]
    scores(0, 0)
    for hh in heads:
        gate = jnp.where(blk_row < i, gates[hh], -jnp.inf)
        sel = jnp.where(blk_row == i, 1.0, 0.0)
        for _ in range(MOBA_TOPK):
            top = jnp.max(gate, axis=0, keepdims=True)
            first = jnp.min(jnp.where(gate == top, blk_row, nblk), axis=0, keepdims=True)
            pick = (blk_row == first) & (top > -jnp.inf)
            sel = jnp.where(pick, 1.0, sel)
            gate = jnp.where(pick, -jnp.inf, gate)
        sel_scr[hh] = sel

    c_far = [rb_ref[hp * HEADS_PER_STEP + hh, far_bucket] * np.float32(LOG2E) for hh in heads]

    def consume(step, slot, carries, far_only):
        new = []
        for hh in heads:
            m, acc = carries[hh]
            xs, ons, vts = [], [], []
            shift = c_far[hh] if far_only else 0.0
            m_new = m
            for t in range(STEP_BLOCKS):
                j = STEP_BLOCKS * step + t
                x = s_scr[slot, hh, t * MOBA_BLOCK:(t + 1) * MOBA_BLOCK]
                if far_only:
                    jc = j
                    on = sel_scr[hh, pl.ds(jc, 1), :] > 0.0
                else:
                    jc = jnp.minimum(j, nblk - 1)
                    kind = jnp.where(j == i, KIND_OWN, jnp.where(j == i - 1, KIND_PREV, KIND_FAR))
                    x = x + bias_scr[hh, kind]
                    on = (sel_scr[hh, pl.ds(jc, 1), :] > 0.0) & (j <= i)
                m_new = jnp.maximum(m_new, jnp.where(on, jnp.max(x, axis=0, keepdims=True) + shift, NEG_BIG))
                xs.append(x), ons.append(on), vts.append(vt_ref[hh, jc])
            ps = [jnp.exp2(x - jnp.where(on, m_new - shift, POS_BIG)).astype(BF16) for x, on in zip(xs, ons)]
            pv = _dot(jnp.concatenate(vts, axis=1), jnp.concatenate(ps, axis=0))
            new.append((m_new, acc * jnp.exp2(m - m_new) + pv))
        return tuple(new)

    def trip(t, carries, first_step, steps, far_only):
        base = first_step + steps * t
        for u in range(steps):
            scores(base + u + 1, (u + 1) % 2)
            carries = consume(base + u, u % 2, carries, far_only)
        return carries

    def run(n, carries, **kw):
        return lax.fori_loop(0, n, functools.partial(trip, **kw), carries)

    n_far = jnp.maximum(i - 1, 0) // (2 * STEP_BLOCKS) * 2
    n_long = n_far // LONG_TRIP_STEPS
    n_all = (i // STEP_BLOCKS + 2) // 2 * 2
    init = (jnp.full((1, tq), NEG_BIG, F32), jnp.zeros((VT_ROWS, tq), F32))
    carries = run(n_long, (init,) * HEADS_PER_STEP, first_step=0, steps=LONG_TRIP_STEPS, far_only=True)
    done = n_long * LONG_TRIP_STEPS
    carries = run((n_far - done) // 2, carries, first_step=done, steps=2, far_only=True)
    carries = run((n_all - n_far) // 2, carries, first_step=n_far, steps=2, far_only=False)
    outs = [acc[:HEAD_DIM] / acc[HEAD_DIM:HEAD_DIM + 1] for _, acc in carries]
    o_ref[...] = jnp.concatenate(outs, axis=0).T.astype(o_ref.dtype)


def _moba_attention(q_gate, q_score, k4, vt4, km, rel_bias):
    s, aw = q_gate.shape
    nblk = s // MOBA_BLOCK
    npair = aw // PAIR_W
    own, prev, far_bucket = _bucket_tiles(s)
    assert nblk % STEP_BLOCKS == 0
    tile = pl.BlockSpec((MOBA_BLOCK, PAIR_W), lambda hp, i: (i, hp))
    return pl.pallas_call(
        functools.partial(_moba_kernel, far_bucket=far_bucket),
        grid=(npair, nblk),
        in_specs=[pl.BlockSpec(memory_space=pltpu.SMEM),
                  tile, tile,
                  pl.BlockSpec((None, nblk, MOBA_BLOCK, PAIR_W), lambda hp, i: (hp, 0, 0, 0)),
                  pl.BlockSpec((HEADS_PER_STEP, nblk, VT_ROWS, MOBA_BLOCK), lambda hp, i: (hp, 0, 0, 0)),
                  pl.BlockSpec((nblk, PAIR_W), lambda hp, i: (0, hp)),
                  pl.BlockSpec((MOBA_BLOCK, MOBA_BLOCK), lambda hp, i: (0, 0)),
                  pl.BlockSpec((MOBA_BLOCK, MOBA_BLOCK), lambda hp, i: (0, 0))],
        out_specs=tile,
        out_shape=jax.ShapeDtypeStruct((s, aw), BF16),
        scratch_shapes=[pltpu.VMEM((HEADS_PER_STEP, 3, MOBA_BLOCK, MOBA_BLOCK), F32),
                        pltpu.VMEM((HEADS_PER_STEP, nblk, MOBA_BLOCK), F32),
                        pltpu.VMEM((2, HEADS_PER_STEP, STEP_BLOCKS * MOBA_BLOCK, MOBA_BLOCK), F32)],
        compiler_params=_params(2),
        name="moba_attn",
    )(rel_bias, q_gate, q_score, k4, vt4, km, jnp.asarray(own), jnp.asarray(prev))


def _proj_kernel(a_ref, x_ref, w_ref, g_ref, b_ref, o_ref):
    y = DEEPNORM_ALPHA * x_ref[...] + _dot(a_ref[...], w_ref[...])
    o_ref[...] = _layer_norm(y, g_ref[...], b_ref[...])


def _proj_ln(a, x, w, g, b):
    s, d = x.shape
    aw = a.shape[1]
    tm = min(MIX_ROWS, s)
    return pl.pallas_call(
        _proj_kernel,
        grid=(s // tm,),
        in_specs=[pl.BlockSpec((tm, aw), lambda i: (i, 0)), pl.BlockSpec((tm, d), lambda i: (i, 0)),
                  _const_spec((aw, d)), _const_spec((1, d)), _const_spec((1, d))],
        out_specs=pl.BlockSpec((tm, d), lambda i: (i, 0)),
        out_shape=jax.ShapeDtypeStruct((s, d), F32),
        compiler_params=_params(),
        name="attn_out_ln",
    )(a, x, w.astype(BF16), g.reshape(1, d), b.reshape(1, d))


def kernel(x, ln_g, ln_b, ffn_w_gate, ffn_w_up, ffn_w_down, gm_w_in, gm_sgu_g, gm_sgu_b, gm_w_s, gm_b_s, gm_w_out, attn_w_q, attn_w_o, w_k_shared, w_v_shared, rel_bias):
    batch, seq, d = x.shape
    assert seq % MOBA_BLOCK == 0 and w_k_shared.shape[1] == N_HEADS * HEAD_DIM
    rows = x.reshape(batch * seq, d)
    outs = []
    for bi in range(batch):
        h = rows[bi * seq:(bi + 1) * seq]
        kv = None
        for layer in range(DEPTH):
            h = _ffn_ln(h, ffn_w_gate[layer, 0], ffn_w_up[layer, 0], ffn_w_down[layer, 0],
                        ln_g[layer, 0], ln_b[layer, 0])
            if layer < N_A_LAYERS:
                a = layer
                h = _gmlp_ln(h, gm_w_in[a], gm_sgu_g[a], gm_sgu_b[a], gm_w_s[a], gm_b_s[a], gm_w_out[a],
                             ln_g[layer, 1], ln_b[layer, 1])
            else:
                j = layer - N_A_LAYERS
                o = _moba_attention(*_q_proj(h, attn_w_q[j]), *kv, rel_bias)
                h = _proj_ln(o, h, attn_w_o[j], ln_g[layer, 1], ln_b[layer, 1])
            h = _ffn_ln(h, ffn_w_gate[layer, 1], ffn_w_up[layer, 1], ffn_w_down[layer, 1],
                        ln_g[layer, 2], ln_b[layer, 2])
            if layer == N_A_LAYERS - 1:
                kv = _shared_kv(h, w_k_shared, w_v_shared)
        outs.append(h)
    return jnp.concatenate(outs, axis=0).reshape(batch, seq, d)
```

```python
import functools
import math

import jax
import jax.numpy as jnp
import numpy as np
from jax import lax
from jax.experimental import pallas as pl
from jax.experimental.pallas import tpu as pltpu

DEPTH = 4
N_A_LAYERS = DEPTH // 2
GM_GROUPS = 8
GM_CHUNK = 128
N_HEADS = 16
HEAD_DIM = 64
MOBA_BLOCK = 256
MOBA_TOPK = 3
REL_BUCKETS = 32
REL_MAX_DIST = 128
DEEPNORM_ALPHA = (2 * DEPTH) ** 0.25
LN_EPS = 1e-5

HEADS_PER_STEP = 2
PAIR_W = HEADS_PER_STEP * HEAD_DIM
VT_ROWS = HEAD_DIM + 16
STEP_BLOCKS = 4
LONG_TRIP_STEPS = 4
KIND_OWN, KIND_PREV, KIND_FAR = 0, 1, 2
LOG2E = math.log2(math.e)
NEG_BIG = -1e30
POS_BIG = 1e30

FFN_ROWS = 512
FFN_CHUNK = 256
MIX_ROWS = 512
VMEM_LIMIT = 56 * 1024 * 1024

F32 = jnp.float32
BF16 = jnp.bfloat16
_NT = (((1,), (1,)), ((), ()))


def _params(n_axes=1):
    return pltpu.CompilerParams(dimension_semantics=("arbitrary",) * n_axes,
                                vmem_limit_bytes=VMEM_LIMIT)


def _const_spec(shape):
    zeros = (0,) * len(shape)
    return pl.BlockSpec(shape, lambda *_: zeros, pipeline_mode=pl.Buffered(1))


def _layer_norm(y, g, b):
    mu = jnp.mean(y, axis=-1, keepdims=True)
    d = y - mu
    var = jnp.mean(d * d, axis=-1, keepdims=True)
    return d * lax.rsqrt(var + LN_EPS) * g + b


def _dot(a, b):
    return jnp.dot(a, b, preferred_element_type=F32)


def _ffn_kernel(x_ref, wg_ref, wu_ref, wd_ref, g_ref, b_ref, o_ref):
    half = x_ref.shape[0] // 2
    ys = []
    for r in range(2):
        x = x_ref[r * half:(r + 1) * half]
        xb = x.astype(BF16)
        h1 = _dot(xb, wg_ref[...])
        h2 = _dot(xb, wu_ref[...])
        a = ((h1 * jax.nn.sigmoid(h1)) * h2).astype(BF16)
        ys.append(DEEPNORM_ALPHA * x + 0.5 * _dot(a, wd_ref[...]))
    for r in range(2):
        o_ref[r * half:(r + 1) * half] = _layer_norm(ys[r], g_ref[...], b_ref[...])


def _ffn_ln(x, w_gate, w_up, w_down, layer, which, g, b):
    s, d = x.shape
    f = w_gate.shape[-1]
    tm = min(FFN_ROWS, s)
    row = pl.BlockSpec((tm, d), lambda i: (i, 0))

    def pick(*shape):
        return pl.BlockSpec((None, None) + shape, lambda i: (layer, which, 0, 0), pipeline_mode=pl.Buffered(1))

    return pl.pallas_call(
        _ffn_kernel,
        grid=(s // tm,),
        in_specs=[row, pick(d, f), pick(d, f), pick(f, d), _const_spec((1, d)), _const_spec((1, d))],
        out_specs=row,
        out_shape=jax.ShapeDtypeStruct((s, d), F32),
        compiler_params=_params(),
        name="ffn_ln",
    )(x, w_gate, w_up, w_down, g.reshape(1, d), b.reshape(1, d))


def _gmlp_kernel(x_ref, win_ref, sg_ref, sb_ref, ws_ref, bs_ref, wout_ref, g_ref, b_ref, o_ref,
                 v_scr, mix_scr):
    tm = x_ref.shape[0]
    w = wout_ref.shape[0]
    gd = w // GM_GROUPS
    nch = tm // GM_CHUNK
    x = x_ref[...]
    z = _dot(x.astype(BF16), win_ref[...])
    z = 0.5 * z * (1.0 + lax.erf(z * np.float32(math.sqrt(0.5))))
    u = z[:, :w]
    v_scr[...] = _layer_norm(z[:, w:], sg_ref[...], sb_ref[...]).astype(BF16)
    t_idx = lax.broadcasted_iota(jnp.int32, (GM_CHUNK, GM_CHUNK), 0)
    s_idx = lax.broadcasted_iota(jnp.int32, (GM_CHUNK, GM_CHUNK), 1)
    for grp in range(GM_GROUPS):
        cols = slice(grp * gd, (grp + 1) * gd)
        w_tril = jnp.where(t_idx >= s_idx, ws_ref[grp], 0.0).astype(BF16)
        vg = jnp.concatenate([v_scr[c * GM_CHUNK:(c + 1) * GM_CHUNK, cols] for c in range(nch)], axis=1)
        mg = _dot(w_tril, vg)
        for c in range(nch):
            mix_scr[c * GM_CHUNK:(c + 1) * GM_CHUNK, cols] = mg[:, c * gd:(c + 1) * gd] + bs_ref[:, cols]
    y = (u * mix_scr[...]).astype(BF16)
    out = DEEPNORM_ALPHA * x + _dot(y, wout_ref[...])
    o_ref[...] = _layer_norm(out, g_ref[...], b_ref[...])


def _gmlp_ln(x, w_in, sgu_g, sgu_b, w_s, b_s, w_out, g, b):
    s, d = x.shape
    w = w_out.shape[0]
    tm = min(MIX_ROWS, s)
    bs_full = jnp.repeat(b_s.T, w // GM_GROUPS, axis=1)
    row = pl.BlockSpec((tm, d), lambda i: (i, 0))
    return pl.pallas_call(
        _gmlp_kernel,
        grid=(s // tm,),
        in_specs=[row, _const_spec((d, 2 * w)), _const_spec((1, w)), _const_spec((1, w)),
                  _const_spec((GM_GROUPS, GM_CHUNK, GM_CHUNK)), _const_spec((GM_CHUNK, w)),
                  _const_spec((w, d)), _const_spec((1, d)), _const_spec((1, d))],
        out_specs=row,
        out_shape=jax.ShapeDtypeStruct((s, d), F32),
        scratch_shapes=[pltpu.VMEM((tm, w), BF16), pltpu.VMEM((tm, w), F32)],
        compiler_params=_params(),
        name="gmlp_ln",
    )(x, w_in.astype(BF16), sgu_g.reshape(1, w), sgu_b.reshape(1, w), w_s, bs_full,
      w_out.astype(BF16), g.reshape(1, d), b.reshape(1, d))


def _kv_kernel(x_ref, wk_ref, wvt_ref, k_ref, vt_ref, km_ref):
    tm = x_ref.shape[0]
    xb = x_ref[...].astype(BF16)
    k = _dot(xb, wk_ref[...])
    vt = lax.dot_general(wvt_ref[...], xb, _NT, preferred_element_type=F32)
    kb = k.astype(BF16)
    vtb = vt.astype(BF16)
    pad_row = lax.broadcasted_iota(jnp.int32, (VT_ROWS - HEAD_DIM, MOBA_BLOCK), 0)
    pad = jnp.where(pad_row == 0, 1.0, 0.0).astype(BF16)
    for blk in range(tm // MOBA_BLOCK):
        rows = slice(blk * MOBA_BLOCK, (blk + 1) * MOBA_BLOCK)
        km_ref[blk] = jnp.mean(k[rows], axis=0, keepdims=True)
        for p in range(k_ref.shape[0]):
            k_ref[p, blk] = kb[rows, p * PAIR_W:(p + 1) * PAIR_W]
        for h in range(vt_ref.shape[0]):
            vt_ref[h, blk] = jnp.concatenate([vtb[h * HEAD_DIM:(h + 1) * HEAD_DIM, rows], pad], axis=0)


def _shared_kv(x, w_k, w_v):
    s, d = x.shape
    aw = w_k.shape[1]
    nblk = s // MOBA_BLOCK
    tm = min(2 * MOBA_BLOCK, s)
    r = tm // MOBA_BLOCK
    npair = aw // PAIR_W
    k4, vt4, km = pl.pallas_call(
        _kv_kernel,
        grid=(s // tm,),
        in_specs=[pl.BlockSpec((tm, d), lambda i: (i, 0)), _const_spec((d, aw)), _const_spec((aw, d))],
        out_specs=[pl.BlockSpec((npair, r, MOBA_BLOCK, PAIR_W), lambda i: (0, i, 0, 0)),
                   pl.BlockSpec((N_HEADS, r, VT_ROWS, MOBA_BLOCK), lambda i: (0, i, 0, 0)),
                   pl.BlockSpec((r, 1, aw), lambda i: (i, 0, 0))],
        out_shape=[jax.ShapeDtypeStruct((npair, nblk, MOBA_BLOCK, PAIR_W), BF16),
                   jax.ShapeDtypeStruct((N_HEADS, nblk, VT_ROWS, MOBA_BLOCK), BF16),
                   jax.ShapeDtypeStruct((nblk, 1, aw), F32)],
        compiler_params=_params(),
        name="shared_kv",
    )(x, w_k.astype(BF16), w_v.T.astype(BF16))
    return k4, vt4, km.reshape(nblk, aw).astype(BF16)


def _t5_bucket_np(dist):
    n = np.maximum(dist, 0)
    max_exact = REL_BUCKETS // 2
    nf = np.maximum(n, 1).astype(np.float32)
    large = max_exact + (np.log(nf / np.float32(max_exact)) / np.float32(math.log(REL_MAX_DIST / max_exact))
                         * np.float32(REL_BUCKETS - max_exact)).astype(np.int32)
    large = np.minimum(large, REL_BUCKETS - 1)
    return np.where(n < max_exact, n, large).astype(np.int32)


def _bucket_tiles(seq):
    kpos = np.arange(MOBA_BLOCK)[:, None]
    qpos = np.arange(MOBA_BLOCK)[None, :]
    own = _t5_bucket_np(qpos - kpos)
    prev = _t5_bucket_np(qpos - kpos + MOBA_BLOCK)
    far = _t5_bucket_np(np.arange(MOBA_BLOCK + 1, max(seq, MOBA_BLOCK + 2)))
    assert (far == far[0]).all(), "bias must be constant beyond the previous block"
    return own, prev, int(far[0])


def _q_kernel(x_ref, wq_ref, qg_ref, qs_ref):
    q = _dot(x_ref[...].astype(BF16), wq_ref[...]) * np.float32(HEAD_DIM ** -0.5)
    qg_ref[...] = q.astype(BF16)
    qs_ref[...] = (q * np.float32(LOG2E)).astype(BF16)


def _q_proj(x, w_q):
    s, d = x.shape
    aw = w_q.shape[1]
    tm = min(MIX_ROWS, s)
    return pl.pallas_call(
        _q_kernel,
        grid=(s // tm,),
        in_specs=[pl.BlockSpec((tm, d), lambda i: (i, 0)), _const_spec((d, aw))],
        out_specs=[pl.BlockSpec((tm, aw), lambda i: (i, 0))] * 2,
        out_shape=[jax.ShapeDtypeStruct((s, aw), BF16)] * 2,
        compiler_params=_params(),
        name="q_proj",
    )(x, w_q.astype(BF16))


def _moba_kernel(rb_ref, qg_ref, qs_ref, k_ref, vt_ref, km_ref, bo_ref, bp_ref, o_ref, bias_scr, sel_scr,
                 s_scr, *, far_bucket):
    hp = pl.program_id(0)
    i = pl.program_id(1)
    tq = qg_ref.shape[0]
    nblk = km_ref.shape[0]
    heads = range(HEADS_PER_STEP)

    @pl.when(i == 0)
    def _():
        kpos = lax.broadcasted_iota(jnp.int32, (MOBA_BLOCK, tq), 0)
        qpos = lax.broadcasted_iota(jnp.int32, (MOBA_BLOCK, tq), 1)
        for hh in heads:
            h = hp * HEADS_PER_STEP + hh
            for kind, bkt_ref in ((KIND_OWN, bo_ref), (KIND_PREV, bp_ref)):
                bkt = bkt_ref[...]
                tile = jnp.zeros(bkt.shape, F32)
                for bucket in range(REL_BUCKETS):
                    tile = jnp.where(bkt == bucket, rb_ref[h, bucket], tile)
                tile = tile * np.float32(LOG2E)
                if kind == KIND_OWN:
                    tile = jnp.where(qpos >= kpos, tile, NEG_BIG)
                bias_scr[hh, kind] = tile
            bias_scr[hh, KIND_FAR] = jnp.full((MOBA_BLOCK, tq), rb_ref[h, far_bucket] * np.float32(LOG2E), F32)

    lane = lax.broadcasted_iota(jnp.int32, (tq, PAIR_W), 1)
    blk_row = lax.broadcasted_iota(jnp.int32, (nblk, tq), 0)

    def head_only(q_pair_ref, hh):
        mine = (lane >= hh * HEAD_DIM) & (lane < (hh + 1) * HEAD_DIM)
        return jnp.where(mine, q_pair_ref[...].astype(F32), 0.0).astype(BF16)

    qss = [head_only(qs_ref, hh) for hh in heads]

    def scores(step, slot):
        j0 = jnp.minimum(STEP_BLOCKS * step, nblk - STEP_BLOCKS)
        ks = k_ref[pl.ds(j0, STEP_BLOCKS)].reshape(STEP_BLOCKS * MOBA_BLOCK, PAIR_W)
        for hh in heads:
            s_scr[slot, hh] = lax.dot_general(ks, qss[hh], _NT, preferred_element_type=F32)

    gates = [lax.dot_general(km_ref[...], head_only(qg_ref, hh), _NT, preferred_element_type=F32)
             for hh in heads]
    scores(0, 0)
    for hh in heads:
        gate = jnp.where(blk_row < i, gates[hh], -jnp.inf)
        sel = jnp.where(blk_row == i, 1.0, 0.0)
        for _ in range(MOBA_TOPK):
            top = jnp.max(gate, axis=0, keepdims=True)
            first = jnp.min(jnp.where(gate == top, blk_row, nblk), axis=0, keepdims=True)
            pick = (blk_row == first) & (top > -jnp.inf)
            sel = jnp.where(pick, 1.0, sel)
            gate = jnp.where(pick, -jnp.inf, gate)
        sel_scr[hh] = sel

    c_far = [rb_ref[hp * HEADS_PER_STEP + hh, far_bucket] * np.float32(LOG2E) for hh in heads]

    def consume(step, slot, carries, far_only):
        new = []
        for hh in heads:
            m, acc = carries[hh]
            xs, ons, vts = [], [], []
            shift = c_far[hh] if far_only else 0.0
            m_new = m
            for t in range(STEP_BLOCKS):
                j = STEP_BLOCKS * step + t
                x = s_scr[slot, hh, t * MOBA_BLOCK:(t + 1) * MOBA_BLOCK]
                if far_only:
                    jc = j
                    on = sel_scr[hh, pl.ds(jc, 1), :] > 0.0
                else:
                    jc = jnp.minimum(j, nblk - 1)
                    kind = jnp.where(j == i, KIND_OWN, jnp.where(j == i - 1, KIND_PREV, KIND_FAR))
                    x = x + bias_scr[hh, kind]
                    on = (sel_scr[hh, pl.ds(jc, 1), :] > 0.0) & (j <= i)
                m_new = jnp.maximum(m_new, jnp.where(on, jnp.max(x, axis=0, keepdims=True) + shift, NEG_BIG))
                xs.append(x), ons.append(on), vts.append(vt_ref[hh, jc])
            ps = [jnp.exp2(x - jnp.where(on, m_new - shift, POS_BIG)).astype(BF16) for x, on in zip(xs, ons)]
            pv = _dot(jnp.concatenate(vts, axis=1), jnp.concatenate(ps, axis=0))
            new.append((m_new, acc * jnp.exp2(m - m_new) + pv))
        return tuple(new)

    def trip(t, carries, first_step, steps, far_only):
        base = first_step + steps * t
        for u in range(steps):
            scores(base + u + 1, (u + 1) % 2)
            carries = consume(base + u, u % 2, carries, far_only)
        return carries

    def run(n, carries, **kw):
        return lax.fori_loop(0, n, functools.partial(trip, **kw), carries)

    n_far = jnp.maximum(i - 1, 0) // (2 * STEP_BLOCKS) * 2
    n_long = n_far // LONG_TRIP_STEPS
    init = (jnp.full((1, tq), NEG_BIG, F32), jnp.zeros((VT_ROWS, tq), F32))
    carries = run(n_long, (init,) * HEADS_PER_STEP, first_step=0, steps=LONG_TRIP_STEPS, far_only=True)
    done = n_long * LONG_TRIP_STEPS
    carries = run((n_far - done) // 2, carries, first_step=done, steps=2, far_only=True)
    n_last = i // STEP_BLOCKS + 1 - n_far
    scores(n_far + 1, 1)
    carries = consume(n_far, 0, carries, False)

    def second(c):
        scores(n_far + 2, 0)
        return consume(n_far + 1, 1, c, False)

    carries = lax.cond(n_last >= 2, second, lambda c: c, carries)
    carries = lax.cond(n_last >= 3, lambda c: consume(n_far + 2, 0, c, False), lambda c: c, carries)
    outs = [acc[:HEAD_DIM] / acc[HEAD_DIM:HEAD_DIM + 1] for _, acc in carries]
    o_ref[...] = jnp.concatenate(outs, axis=0).T.astype(o_ref.dtype)


def _moba_attention(q_gate, q_score, k4, vt4, km, rel_bias):
    s, aw = q_gate.shape
    nblk = s // MOBA_BLOCK
    npair = aw // PAIR_W
    own, prev, far_bucket = _bucket_tiles(s)
    assert nblk % STEP_BLOCKS == 0
    tile = pl.BlockSpec((MOBA_BLOCK, PAIR_W), lambda hp, i: (i, hp))
    return pl.pallas_call(
        functools.partial(_moba_kernel, far_bucket=far_bucket),
        grid=(npair, nblk),
        in_specs=[pl.BlockSpec(memory_space=pltpu.SMEM),
                  tile, tile,
                  pl.BlockSpec((None, nblk, MOBA_BLOCK, PAIR_W), lambda hp, i: (hp, 0, 0, 0)),
                  pl.BlockSpec((HEADS_PER_STEP, nblk, VT_ROWS, MOBA_BLOCK), lambda hp, i: (hp, 0, 0, 0)),
                  pl.BlockSpec((nblk, PAIR_W), lambda hp, i: (0, hp)),
                  pl.BlockSpec((MOBA_BLOCK, MOBA_BLOCK), lambda hp, i: (0, 0)),
                  pl.BlockSpec((MOBA_BLOCK, MOBA_BLOCK), lambda hp, i: (0, 0))],
        out_specs=tile,
        out_shape=jax.ShapeDtypeStruct((s, aw), BF16),
        scratch_shapes=[pltpu.VMEM((HEADS_PER_STEP, 3, MOBA_BLOCK, MOBA_BLOCK), F32),
                        pltpu.VMEM((HEADS_PER_STEP, nblk, MOBA_BLOCK), F32),
                        pltpu.VMEM((2, HEADS_PER_STEP, STEP_BLOCKS * MOBA_BLOCK, MOBA_BLOCK), F32)],
        compiler_params=_params(2),
        name="moba_attn",
    )(rel_bias, q_gate, q_score, k4, vt4, km, jnp.asarray(own), jnp.asarray(prev))


def _proj_kernel(a_ref, x_ref, w_ref, g_ref, b_ref, o_ref):
    y = DEEPNORM_ALPHA * x_ref[...] + _dot(a_ref[...], w_ref[...])
    o_ref[...] = _layer_norm(y, g_ref[...], b_ref[...])


def _proj_ln(a, x, w, g, b):
    s, d = x.shape
    aw = a.shape[1]
    tm = min(MIX_ROWS, s)
    return pl.pallas_call(
        _proj_kernel,
        grid=(s // tm,),
        in_specs=[pl.BlockSpec((tm, aw), lambda i: (i, 0)), pl.BlockSpec((tm, d), lambda i: (i, 0)),
                  _const_spec((aw, d)), _const_spec((1, d)), _const_spec((1, d))],
        out_specs=pl.BlockSpec((tm, d), lambda i: (i, 0)),
        out_shape=jax.ShapeDtypeStruct((s, d), F32),
        compiler_params=_params(),
        name="attn_out_ln",
    )(a, x, w.astype(BF16), g.reshape(1, d), b.reshape(1, d))


def kernel(x, ln_g, ln_b, ffn_w_gate, ffn_w_up, ffn_w_down, gm_w_in, gm_sgu_g, gm_sgu_b, gm_w_s, gm_b_s, gm_w_out, attn_w_q, attn_w_o, w_k_shared, w_v_shared, rel_bias):
    batch, seq, d = x.shape
    assert seq % MOBA_BLOCK == 0 and w_k_shared.shape[1] == N_HEADS * HEAD_DIM
    rows = x.reshape(batch * seq, d)
    ffn_w = (ffn_w_gate.astype(BF16), ffn_w_up.astype(BF16), ffn_w_down.astype(BF16))
    outs = []
    for bi in range(batch):
        h = rows[bi * seq:(bi + 1) * seq]
        kv = None
        for layer in range(DEPTH):
            h = _ffn_ln(h, *ffn_w, layer, 0, ln_g[layer, 0], ln_b[layer, 0])
            if layer < N_A_LAYERS:
                a = layer
                h = _gmlp_ln(h, gm_w_in[a], gm_sgu_g[a], gm_sgu_b[a], gm_w_s[a], gm_b_s[a], gm_w_out[a],
                             ln_g[layer, 1], ln_b[layer, 1])
            else:
                j = layer - N_A_LAYERS
                o = _moba_attention(*_q_proj(h, attn_w_q[j]), *kv, rel_bias)
                h = _proj_ln(o, h, attn_w_o[j], ln_g[layer, 1], ln_b[layer, 1])
            h = _ffn_ln(h, *ffn_w, layer, 1, ln_g[layer, 2], ln_b[layer, 2])
            if layer == N_A_LAYERS - 1:
                kv = _shared_kv(h, w_k_shared, w_v_shared)
        outs.append(h)
    return jnp.concatenate(outs, axis=0).reshape(batch, seq, d)
```

```python
import functools
import math

import jax
import jax.numpy as jnp
import numpy as np
from jax import lax
from jax.experimental import pallas as pl
from jax.experimental.pallas import tpu as pltpu

DEPTH = 4
N_A_LAYERS = DEPTH // 2
GM_GROUPS = 8
GM_CHUNK = 128
N_HEADS = 16
HEAD_DIM = 64
MOBA_BLOCK = 256
MOBA_TOPK = 3
REL_BUCKETS = 32
REL_MAX_DIST = 128
DEEPNORM_ALPHA = (2 * DEPTH) ** 0.25
LN_EPS = 1e-5

PAIR_HEADS = 2
PAIR_W = PAIR_HEADS * HEAD_DIM
PAIRS_PER_STEP = 2
HEADS_PER_STEP = PAIRS_PER_STEP * PAIR_HEADS
STEP_W = PAIRS_PER_STEP * PAIR_W
VT_ROWS = HEAD_DIM + 16
STEP_BLOCKS = 4
TRIP_STEPS = (4, 2)
KIND_OWN, KIND_PREV, KIND_FAR = 0, 1, 2
LOG2E = math.log2(math.e)
NEG_BIG = -1e30
POS_BIG = 1e30

FFN_ROWS = 512
FFN_CHUNK = 256
MIX_ROWS = 512
VMEM_LIMIT = 56 * 1024 * 1024

F32 = jnp.float32
BF16 = jnp.bfloat16
_NT = (((1,), (1,)), ((), ()))


def _params(n_axes=1):
    return pltpu.CompilerParams(dimension_semantics=("arbitrary",) * n_axes,
                                vmem_limit_bytes=VMEM_LIMIT)


def _const_spec(shape):
    zeros = (0,) * len(shape)
    return pl.BlockSpec(shape, lambda *_: zeros, pipeline_mode=pl.Buffered(1))


def _layer_norm(y, g, b):
    mu = jnp.mean(y, axis=-1, keepdims=True)
    d = y - mu
    var = jnp.mean(d * d, axis=-1, keepdims=True)
    return d * lax.rsqrt(var + LN_EPS) * g + b


def _dot(a, b):
    return jnp.dot(a, b, preferred_element_type=F32)


def _ffn_kernel(x_ref, wg_ref, wu_ref, wd_ref, g_ref, b_ref, o_ref):
    half = x_ref.shape[0] // 2
    ys = []
    for r in range(2):
        x = x_ref[r * half:(r + 1) * half]
        xb = x.astype(BF16)
        h1 = _dot(xb, wg_ref[...])
        h2 = _dot(xb, wu_ref[...])
        a = ((h1 * jax.nn.sigmoid(h1)) * h2).astype(BF16)
        ys.append(DEEPNORM_ALPHA * x + 0.5 * _dot(a, wd_ref[...]))
    for r in range(2):
        o_ref[r * half:(r + 1) * half] = _layer_norm(ys[r], g_ref[...], b_ref[...])


def _ffn_ln(x, w_gate, w_up, w_down, layer, which, g, b):
    s, d = x.shape
    f = w_gate.shape[-1]
    tm = min(FFN_ROWS, s)
    row = pl.BlockSpec((tm, d), lambda i: (i, 0))

    def pick(*shape):
        return pl.BlockSpec((None, None) + shape, lambda i: (layer, which, 0, 0), pipeline_mode=pl.Buffered(1))

    return pl.pallas_call(
        _ffn_kernel,
        grid=(s // tm,),
        in_specs=[row, pick(d, f), pick(d, f), pick(f, d), _const_spec((1, d)), _const_spec((1, d))],
        out_specs=row,
        out_shape=jax.ShapeDtypeStruct((s, d), F32),
        compiler_params=_params(),
        name="ffn_ln",
    )(x, w_gate, w_up, w_down, g.reshape(1, d), b.reshape(1, d))


def _gmlp_kernel(x_ref, win_ref, sg_ref, sb_ref, ws_ref, bs_ref, wout_ref, g_ref, b_ref, o_ref,
                 v_scr, mix_scr):
    tm = x_ref.shape[0]
    w = wout_ref.shape[0]
    gd = w // GM_GROUPS
    nch = tm // GM_CHUNK
    x = x_ref[...]
    z = _dot(x.astype(BF16), win_ref[...])
    z = 0.5 * z * (1.0 + lax.erf(z * np.float32(math.sqrt(0.5))))
    u = z[:, :w]
    v_scr[...] = _layer_norm(z[:, w:], sg_ref[...], sb_ref[...]).astype(BF16)
    t_idx = lax.broadcasted_iota(jnp.int32, (GM_CHUNK, GM_CHUNK), 0)
    s_idx = lax.broadcasted_iota(jnp.int32, (GM_CHUNK, GM_CHUNK), 1)
    for grp in range(GM_GROUPS):
        cols = slice(grp * gd, (grp + 1) * gd)
        w_tril = jnp.where(t_idx >= s_idx, ws_ref[grp], 0.0).astype(BF16)
        vg = jnp.concatenate([v_scr[c * GM_CHUNK:(c + 1) * GM_CHUNK, cols] for c in range(nch)], axis=1)
        mg = _dot(w_tril, vg)
        for c in range(nch):
            mix_scr[c * GM_CHUNK:(c + 1) * GM_CHUNK, cols] = mg[:, c * gd:(c + 1) * gd] + bs_ref[:, cols]
    y = (u * mix_scr[...]).astype(BF16)
    out = DEEPNORM_ALPHA * x + _dot(y, wout_ref[...])
    o_ref[...] = _layer_norm(out, g_ref[...], b_ref[...])


def _gmlp_ln(x, w_in, sgu_g, sgu_b, w_s, b_s, w_out, g, b):
    s, d = x.shape
    w = w_out.shape[0]
    tm = min(MIX_ROWS, s)
    bs_full = jnp.repeat(b_s.T, w // GM_GROUPS, axis=1)
    row = pl.BlockSpec((tm, d), lambda i: (i, 0))
    return pl.pallas_call(
        _gmlp_kernel,
        grid=(s // tm,),
        in_specs=[row, _const_spec((d, 2 * w)), _const_spec((1, w)), _const_spec((1, w)),
                  _const_spec((GM_GROUPS, GM_CHUNK, GM_CHUNK)), _const_spec((GM_CHUNK, w)),
                  _const_spec((w, d)), _const_spec((1, d)), _const_spec((1, d))],
        out_specs=row,
        out_shape=jax.ShapeDtypeStruct((s, d), F32),
        scratch_shapes=[pltpu.VMEM((tm, w), BF16), pltpu.VMEM((tm, w), F32)],
        compiler_params=_params(),
        name="gmlp_ln",
    )(x, w_in.astype(BF16), sgu_g.reshape(1, w), sgu_b.reshape(1, w), w_s, bs_full,
      w_out.astype(BF16), g.reshape(1, d), b.reshape(1, d))


def _kv_kernel(x_ref, wk_ref, wvt_ref, k_ref, vt_ref, km_ref):
    tm = x_ref.shape[0]
    xb = x_ref[...].astype(BF16)
    k = _dot(xb, wk_ref[...])
    vt = lax.dot_general(wvt_ref[...], xb, _NT, preferred_element_type=F32)
    kb = k.astype(BF16)
    vtb = vt.astype(BF16)
    pad_row = lax.broadcasted_iota(jnp.int32, (VT_ROWS - HEAD_DIM, MOBA_BLOCK), 0)
    pad = jnp.where(pad_row == 0, 1.0, 0.0).astype(BF16)
    for blk in range(tm // MOBA_BLOCK):
        rows = slice(blk * MOBA_BLOCK, (blk + 1) * MOBA_BLOCK)
        km_ref[blk] = jnp.mean(k[rows], axis=0, keepdims=True)
        for p in range(k_ref.shape[0]):
            k_ref[p, blk] = kb[rows, p * PAIR_W:(p + 1) * PAIR_W]
        for h in range(vt_ref.shape[0]):
            vt_ref[h, blk] = jnp.concatenate([vtb[h * HEAD_DIM:(h + 1) * HEAD_DIM, rows], pad], axis=0)


def _shared_kv(x, w_k, w_v):
    s, d = x.shape
    aw = w_k.shape[1]
    nblk = s // MOBA_BLOCK
    tm = min(2 * MOBA_BLOCK, s)
    r = tm // MOBA_BLOCK
    npair = aw // PAIR_W
    k4, vt4, km = pl.pallas_call(
        _kv_kernel,
        grid=(s // tm,),
        in_specs=[pl.BlockSpec((tm, d), lambda i: (i, 0)), _const_spec((d, aw)), _const_spec((aw, d))],
        out_specs=[pl.BlockSpec((npair, r, MOBA_BLOCK, PAIR_W), lambda i: (0, i, 0, 0)),
                   pl.BlockSpec((N_HEADS, r, VT_ROWS, MOBA_BLOCK), lambda i: (0, i, 0, 0)),
                   pl.BlockSpec((r, 1, aw), lambda i: (i, 0, 0))],
        out_shape=[jax.ShapeDtypeStruct((npair, nblk, MOBA_BLOCK, PAIR_W), BF16),
                   jax.ShapeDtypeStruct((N_HEADS, nblk, VT_ROWS, MOBA_BLOCK), BF16),
                   jax.ShapeDtypeStruct((nblk, 1, aw), F32)],
        compiler_params=_params(),
        name="shared_kv",
    )(x, w_k.astype(BF16), w_v.T.astype(BF16))
    return k4, vt4, km.reshape(nblk, aw).astype(BF16)


def _t5_bucket_np(dist):
    n = np.maximum(dist, 0)
    max_exact = REL_BUCKETS // 2
    nf = np.maximum(n, 1).astype(np.float32)
    large = max_exact + (np.log(nf / np.float32(max_exact)) / np.float32(math.log(REL_MAX_DIST / max_exact))
                         * np.float32(REL_BUCKETS - max_exact)).astype(np.int32)
    large = np.minimum(large, REL_BUCKETS - 1)
    return np.where(n < max_exact, n, large).astype(np.int32)


def _bucket_tiles(seq):
    kpos = np.arange(MOBA_BLOCK)[:, None]
    qpos = np.arange(MOBA_BLOCK)[None, :]
    own = _t5_bucket_np(qpos - kpos)
    prev = _t5_bucket_np(qpos - kpos + MOBA_BLOCK)
    far = _t5_bucket_np(np.arange(MOBA_BLOCK + 1, max(seq, MOBA_BLOCK + 2)))
    assert (far == far[0]).all(), "bias must be constant beyond the previous block"
    return own, prev, int(far[0])


def _q_kernel(x_ref, wq_ref, qg_ref, qs_ref):
    q = _dot(x_ref[...].astype(BF16), wq_ref[...]) * np.float32(HEAD_DIM ** -0.5)
    qg_ref[...] = q.astype(BF16)
    qs_ref[...] = (q * np.float32(LOG2E)).astype(BF16)


def _q_proj(x, w_q):
    s, d = x.shape
    aw = w_q.shape[1]
    tm = min(MIX_ROWS, s)
    return pl.pallas_call(
        _q_kernel,
        grid=(s // tm,),
        in_specs=[pl.BlockSpec((tm, d), lambda i: (i, 0)), _const_spec((d, aw))],
        out_specs=[pl.BlockSpec((tm, aw), lambda i: (i, 0))] * 2,
        out_shape=[jax.ShapeDtypeStruct((s, aw), BF16)] * 2,
        compiler_params=_params(),
        name="q_proj",
    )(x, w_q.astype(BF16))


def _moba_kernel(rb_ref, qg_ref, qs_ref, k_ref, vt_ref, km_ref, bo_ref, bp_ref, o_ref, bias_scr, sel_scr,
                 s_scr, *, far_bucket):
    hp = pl.program_id(0)
    i = pl.program_id(1)
    tq = qg_ref.shape[0]
    nblk = km_ref.shape[0]
    heads = range(HEADS_PER_STEP)

    @pl.when(i == 0)
    def _():
        kpos = lax.broadcasted_iota(jnp.int32, (MOBA_BLOCK, tq), 0)
        qpos = lax.broadcasted_iota(jnp.int32, (MOBA_BLOCK, tq), 1)
        for hh in heads:
            h = hp * HEADS_PER_STEP + hh
            for kind, bkt_ref in ((KIND_OWN, bo_ref), (KIND_PREV, bp_ref)):
                bkt = bkt_ref[...]
                tile = jnp.zeros(bkt.shape, F32)
                for bucket in range(REL_BUCKETS):
                    tile = jnp.where(bkt == bucket, rb_ref[h, bucket], tile)
                tile = tile * np.float32(LOG2E)
                if kind == KIND_OWN:
                    tile = jnp.where(qpos >= kpos, tile, NEG_BIG)
                bias_scr[hh, kind] = tile
            bias_scr[hh, KIND_FAR] = jnp.full((MOBA_BLOCK, tq), rb_ref[h, far_bucket] * np.float32(LOG2E), F32)

    lane = lax.broadcasted_iota(jnp.int32, (tq, PAIR_W), 1)
    blk_row = lax.broadcasted_iota(jnp.int32, (nblk, tq), 0)

    def pair_cols(hh):
        pair = hh // PAIR_HEADS
        return slice(pair * PAIR_W, (pair + 1) * PAIR_W)

    def head_only(q_ref, hh):
        lo = (hh % PAIR_HEADS) * HEAD_DIM
        mine = (lane >= lo) & (lane < lo + HEAD_DIM)
        return jnp.where(mine, q_ref[:, pair_cols(hh)].astype(F32), 0.0).astype(BF16)

    qss = [head_only(qs_ref, hh) for hh in heads]

    def scores(step, slot):
        j0 = jnp.minimum(STEP_BLOCKS * step, nblk - STEP_BLOCKS)
        for pair in range(PAIRS_PER_STEP):
            ks = k_ref[pair, pl.ds(j0, STEP_BLOCKS)].reshape(STEP_BLOCKS * MOBA_BLOCK, PAIR_W)
            for hh in range(pair * PAIR_HEADS, (pair + 1) * PAIR_HEADS):
                s_scr[slot, hh] = lax.dot_general(ks, qss[hh], _NT, preferred_element_type=F32)

    gates = [lax.dot_general(km_ref[:, pair_cols(hh)], head_only(qg_ref, hh), _NT, preferred_element_type=F32)
             for hh in heads]
    scores(0, 0)
    for hh in heads:
        gate = jnp.where(blk_row < i, gates[hh], -jnp.inf)
        sel = jnp.where(blk_row == i, 1.0, 0.0)
        for _ in range(MOBA_TOPK):
            top = jnp.max(gate, axis=0, keepdims=True)
            first = jnp.min(jnp.where(gate == top, blk_row, nblk), axis=0, keepdims=True)
            pick = (blk_row == first) & (top > -jnp.inf)
            sel = jnp.where(pick, 1.0, sel)
            gate = jnp.where(pick, -jnp.inf, gate)
        sel_scr[hh] = sel

    c_far = [rb_ref[hp * HEADS_PER_STEP + hh, far_bucket] * np.float32(LOG2E) for hh in heads]

    def consume(step, slot, carries, far_only):
        new = []
        for hh in heads:
            m, acc = carries[hh]
            xs, ons, vts = [], [], []
            shift = c_far[hh] if far_only else 0.0
            m_new = m
            for t in range(STEP_BLOCKS):
                j = STEP_BLOCKS * step + t
                x = s_scr[slot, hh, t * MOBA_BLOCK:(t + 1) * MOBA_BLOCK]
                if far_only:
                    jc = j
                    on = sel_scr[hh, pl.ds(jc, 1), :] > 0.0
                else:
                    jc = jnp.minimum(j, nblk - 1)
                    kind = jnp.where(j == i, KIND_OWN, jnp.where(j == i - 1, KIND_PREV, KIND_FAR))
                    x = x + bias_scr[hh, kind]
                    on = (sel_scr[hh, pl.ds(jc, 1), :] > 0.0) & (j <= i)
                m_new = jnp.maximum(m_new, jnp.where(on, jnp.max(x, axis=0, keepdims=True) + shift, NEG_BIG))
                xs.append(x), ons.append(on), vts.append(vt_ref[hh, jc])
            ps = [jnp.exp2(x - jnp.where(on, m_new - shift, POS_BIG)).astype(BF16) for x, on in zip(xs, ons)]
            pv = _dot(jnp.concatenate(vts, axis=1), jnp.concatenate(ps, axis=0))
            new.append((m_new, acc * jnp.exp2(m - m_new) + pv))
        return tuple(new)

    def trip(t, carries, first_step, steps, far_only):
        base = first_step + steps * t
        for u in range(steps):
            scores(base + u + 1, (u + 1) % 2)
            carries = consume(base + u, u % 2, carries, far_only)
        return carries

    def run(n, carries, **kw):
        return lax.fori_loop(0, n, functools.partial(trip, **kw), carries)

    n_far = jnp.maximum(i - 1, 0) // (2 * STEP_BLOCKS) * 2
    init = (jnp.full((1, tq), NEG_BIG, F32), jnp.zeros((VT_ROWS, tq), F32))
    carries = (init,) * HEADS_PER_STEP
    done = 0
    for steps in TRIP_STEPS:
        n = (n_far - done) // steps
        carries = run(n, carries, first_step=done, steps=steps, far_only=True)
        done = done + n * steps
    n_last = i // STEP_BLOCKS + 1 - n_far
    scores(n_far + 1, 1)
    carries = consume(n_far, 0, carries, False)

    def second(c):
        scores(n_far + 2, 0)
        return consume(n_far + 1, 1, c, False)

    carries = lax.cond(n_last >= 2, second, lambda c: c, carries)
    carries = lax.cond(n_last >= 3, lambda c: consume(n_far + 2, 0, c, False), lambda c: c, carries)
    outs = [acc[:HEAD_DIM] / acc[HEAD_DIM:HEAD_DIM + 1] for _, acc in carries]
    o_ref[...] = jnp.concatenate(outs, axis=0).T.astype(o_ref.dtype)


def _moba_attention(q_gate, q_score, k4, vt4, km, rel_bias):
    s, aw = q_gate.shape
    nblk = s // MOBA_BLOCK
    own, prev, far_bucket = _bucket_tiles(s)
    assert nblk % STEP_BLOCKS == 0 and aw % STEP_W == 0
    tile = pl.BlockSpec((MOBA_BLOCK, STEP_W), lambda hp, i: (i, hp))
    once = dict(pipeline_mode=pl.Buffered(1))
    return pl.pallas_call(
        functools.partial(_moba_kernel, far_bucket=far_bucket),
        grid=(aw // STEP_W, nblk),
        in_specs=[pl.BlockSpec(memory_space=pltpu.SMEM),
                  tile, tile,
                  pl.BlockSpec((PAIRS_PER_STEP, nblk, MOBA_BLOCK, PAIR_W), lambda hp, i: (hp, 0, 0, 0), **once),
                  pl.BlockSpec((HEADS_PER_STEP, nblk, VT_ROWS, MOBA_BLOCK), lambda hp, i: (hp, 0, 0, 0), **once),
                  pl.BlockSpec((nblk, STEP_W), lambda hp, i: (0, hp)),
                  pl.BlockSpec((MOBA_BLOCK, MOBA_BLOCK), lambda hp, i: (0, 0)),
                  pl.BlockSpec((MOBA_BLOCK, MOBA_BLOCK), lambda hp, i: (0, 0))],
        out_specs=tile,
        out_shape=jax.ShapeDtypeStruct((s, aw), BF16),
        scratch_shapes=[pltpu.VMEM((HEADS_PER_STEP, 3, MOBA_BLOCK, MOBA_BLOCK), F32),
                        pltpu.VMEM((HEADS_PER_STEP, nblk, MOBA_BLOCK), F32),
                        pltpu.VMEM((2, HEADS_PER_STEP, STEP_BLOCKS * MOBA_BLOCK, MOBA_BLOCK), F32)],
        compiler_params=_params(2),
        name="moba_attn",
    )(rel_bias, q_gate, q_score, k4, vt4, km, jnp.asarray(own), jnp.asarray(prev))


def _proj_kernel(a_ref, x_ref, w_ref, g_ref, b_ref, o_ref):
    y = DEEPNORM_ALPHA * x_ref[...] + _dot(a_ref[...], w_ref[...])
    o_ref[...] = _layer_norm(y, g_ref[...], b_ref[...])


def _proj_ln(a, x, w, g, b):
    s, d = x.shape
    aw = a.shape[1]
    tm = min(MIX_ROWS, s)
    return pl.pallas_call(
        _proj_kernel,
        grid=(s // tm,),
        in_specs=[pl.BlockSpec((tm, aw), lambda i: (i, 0)), pl.BlockSpec((tm, d), lambda i: (i, 0)),
                  _const_spec((aw, d)), _const_spec((1, d)), _const_spec((1, d))],
        out_specs=pl.BlockSpec((tm, d), lambda i: (i, 0)),
        out_shape=jax.ShapeDtypeStruct((s, d), F32),
        compiler_params=_params(),
        name="attn_out_ln",
    )(a, x, w.astype(BF16), g.reshape(1, d), b.reshape(1, d))


def kernel(x, ln_g, ln_b, ffn_w_gate, ffn_w_up, ffn_w_down, gm_w_in, gm_sgu_g, gm_sgu_b, gm_w_s, gm_b_s, gm_w_out, attn_w_q, attn_w_o, w_k_shared, w_v_shared, rel_bias):
    batch, seq, d = x.shape
    assert seq % MOBA_BLOCK == 0 and w_k_shared.shape[1] == N_HEADS * HEAD_DIM
    rows = x.reshape(batch * seq, d)
    ffn_w = (ffn_w_gate.astype(BF16), ffn_w_up.astype(BF16), ffn_w_down.astype(BF16))
    outs = []
    for bi in range(batch):
        h = rows[bi * seq:(bi + 1) * seq]
        kv = None
        for layer in range(DEPTH):
            h = _ffn_ln(h, *ffn_w, layer, 0, ln_g[layer, 0], ln_b[layer, 0])
            if layer < N_A_LAYERS:
                a = layer
                h = _gmlp_ln(h, gm_w_in[a], gm_sgu_g[a], gm_sgu_b[a], gm_w_s[a], gm_b_s[a], gm_w_out[a],
                             ln_g[layer, 1], ln_b[layer, 1])
            else:
                j = layer - N_A_LAYERS
                o = _moba_attention(*_q_proj(h, attn_w_q[j]), *kv, rel_bias)
                h = _proj_ln(o, h, attn_w_o[j], ln_g[layer, 1], ln_b[layer, 1])
            h = _ffn_ln(h, *ffn_w, layer, 1, ln_g[layer, 2], ln_b[layer, 2])
            if layer == N_A_LAYERS - 1:
                kv = _shared_kv(h, w_k_shared, w_v_shared)
        outs.append(h)
    return jnp.concatenate(outs, axis=0).reshape(batch, seq, d)
```

```python
import functools
import math

import jax
import jax.numpy as jnp
import numpy as np
from jax import lax
from jax.experimental import pallas as pl
from jax.experimental.pallas import tpu as pltpu

DEPTH = 4
N_A_LAYERS = DEPTH // 2
GM_GROUPS = 8
GM_CHUNK = 128
N_HEADS = 16
HEAD_DIM = 64
MOBA_BLOCK = 256
MOBA_TOPK = 3
REL_BUCKETS = 32
REL_MAX_DIST = 128
DEEPNORM_ALPHA = (2 * DEPTH) ** 0.25
LN_EPS = 1e-5

PAIR_HEADS = 2
PAIR_W = PAIR_HEADS * HEAD_DIM
PAIRS_PER_STEP = 2
HEADS_PER_STEP = PAIRS_PER_STEP * PAIR_HEADS
STEP_W = PAIRS_PER_STEP * PAIR_W
VT_ROWS = HEAD_DIM + 16
STEP_BLOCKS = 2
TRIP_STEPS = (8, 4, 2)
KIND_OWN, KIND_PREV, KIND_FAR = 0, 1, 2
LOG2E = math.log2(math.e)
NEG_BIG = -1e30
POS_BIG = 1e30

FFN_ROWS = 512
FFN_CHUNK = 256
MIX_ROWS = 512
VMEM_LIMIT = 56 * 1024 * 1024

F32 = jnp.float32
BF16 = jnp.bfloat16
_NT = (((1,), (1,)), ((), ()))


def _params(n_axes=1):
    return pltpu.CompilerParams(dimension_semantics=("arbitrary",) * n_axes,
                                vmem_limit_bytes=VMEM_LIMIT)


def _const_spec(shape):
    zeros = (0,) * len(shape)
    return pl.BlockSpec(shape, lambda *_: zeros, pipeline_mode=pl.Buffered(1))


def _layer_norm(y, g, b):
    mu = jnp.mean(y, axis=-1, keepdims=True)
    d = y - mu
    var = jnp.mean(d * d, axis=-1, keepdims=True)
    return d * lax.rsqrt(var + LN_EPS) * g + b


def _dot(a, b):
    return jnp.dot(a, b, preferred_element_type=F32)


def _ffn_kernel(x_ref, wg_ref, wu_ref, wd_ref, g_ref, b_ref, o_ref):
    half = x_ref.shape[0] // 2
    ys = []
    for r in range(2):
        x = x_ref[r * half:(r + 1) * half]
        xb = x.astype(BF16)
        h1 = _dot(xb, wg_ref[...])
        h2 = _dot(xb, wu_ref[...])
        a = ((h1 * jax.nn.sigmoid(h1)) * h2).astype(BF16)
        ys.append(DEEPNORM_ALPHA * x + 0.5 * _dot(a, wd_ref[...]))
    for r in range(2):
        o_ref[r * half:(r + 1) * half] = _layer_norm(ys[r], g_ref[...], b_ref[...])


def _ffn_ln(x, w_gate, w_up, w_down, layer, which, g, b):
    s, d = x.shape
    f = w_gate.shape[-1]
    tm = min(FFN_ROWS, s)
    row = pl.BlockSpec((tm, d), lambda i: (i, 0))

    def pick(*shape):
        return pl.BlockSpec((None, None) + shape, lambda i: (layer, which, 0, 0), pipeline_mode=pl.Buffered(1))

    return pl.pallas_call(
        _ffn_kernel,
        grid=(s // tm,),
        in_specs=[row, pick(d, f), pick(d, f), pick(f, d), _const_spec((1, d)), _const_spec((1, d))],
        out_specs=row,
        out_shape=jax.ShapeDtypeStruct((s, d), F32),
        compiler_params=_params(),
        name="ffn_ln",
    )(x, w_gate, w_up, w_down, g.reshape(1, d), b.reshape(1, d))


def _gmlp_kernel(x_ref, win_ref, sg_ref, sb_ref, ws_ref, bs_ref, wout_ref, g_ref, b_ref, o_ref,
                 v_scr, mix_scr):
    tm = x_ref.shape[0]
    w = wout_ref.shape[0]
    gd = w // GM_GROUPS
    nch = tm // GM_CHUNK
    x = x_ref[...]
    z = _dot(x.astype(BF16), win_ref[...])
    z = 0.5 * z * (1.0 + lax.erf(z * np.float32(math.sqrt(0.5))))
    u = z[:, :w]
    v_scr[...] = _layer_norm(z[:, w:], sg_ref[...], sb_ref[...]).astype(BF16)
    t_idx = lax.broadcasted_iota(jnp.int32, (GM_CHUNK, GM_CHUNK), 0)
    s_idx = lax.broadcasted_iota(jnp.int32, (GM_CHUNK, GM_CHUNK), 1)
    for grp in range(GM_GROUPS):
        cols = slice(grp * gd, (grp + 1) * gd)
        w_tril = jnp.where(t_idx >= s_idx, ws_ref[grp], 0.0).astype(BF16)
        vg = jnp.concatenate([v_scr[c * GM_CHUNK:(c + 1) * GM_CHUNK, cols] for c in range(nch)], axis=1)
        mg = _dot(w_tril, vg)
        for c in range(nch):
            mix_scr[c * GM_CHUNK:(c + 1) * GM_CHUNK, cols] = mg[:, c * gd:(c + 1) * gd] + bs_ref[:, cols]
    y = (u * mix_scr[...]).astype(BF16)
    out = DEEPNORM_ALPHA * x + _dot(y, wout_ref[...])
    o_ref[...] = _layer_norm(out, g_ref[...], b_ref[...])


def _gmlp_ln(x, w_in, sgu_g, sgu_b, w_s, b_s, w_out, g, b):
    s, d = x.shape
    w = w_out.shape[0]
    tm = min(MIX_ROWS, s)
    bs_full = jnp.repeat(b_s.T, w // GM_GROUPS, axis=1)
    row = pl.BlockSpec((tm, d), lambda i: (i, 0))
    return pl.pallas_call(
        _gmlp_kernel,
        grid=(s // tm,),
        in_specs=[row, _const_spec((d, 2 * w)), _const_spec((1, w)), _const_spec((1, w)),
                  _const_spec((GM_GROUPS, GM_CHUNK, GM_CHUNK)), _const_spec((GM_CHUNK, w)),
                  _const_spec((w, d)), _const_spec((1, d)), _const_spec((1, d))],
        out_specs=row,
        out_shape=jax.ShapeDtypeStruct((s, d), F32),
        scratch_shapes=[pltpu.VMEM((tm, w), BF16), pltpu.VMEM((tm, w), F32)],
        compiler_params=_params(),
        name="gmlp_ln",
    )(x, w_in.astype(BF16), sgu_g.reshape(1, w), sgu_b.reshape(1, w), w_s, bs_full,
      w_out.astype(BF16), g.reshape(1, d), b.reshape(1, d))


def _kv_kernel(x_ref, wk_ref, wvt_ref, k_ref, vt_ref, km_ref):
    tm = x_ref.shape[0]
    xb = x_ref[...].astype(BF16)
    k = _dot(xb, wk_ref[...])
    vt = lax.dot_general(wvt_ref[...], xb, _NT, preferred_element_type=F32)
    kb = k.astype(BF16)
    vtb = vt.astype(BF16)
    pad_row = lax.broadcasted_iota(jnp.int32, (VT_ROWS - HEAD_DIM, MOBA_BLOCK), 0)
    pad = jnp.where(pad_row == 0, 1.0, 0.0).astype(BF16)
    for blk in range(tm // MOBA_BLOCK):
        rows = slice(blk * MOBA_BLOCK, (blk + 1) * MOBA_BLOCK)
        km_ref[blk] = jnp.mean(k[rows], axis=0, keepdims=True)
        for p in range(k_ref.shape[0]):
            k_ref[p, blk] = kb[rows, p * PAIR_W:(p + 1) * PAIR_W]
        for h in range(vt_ref.shape[0]):
            vt_ref[h, blk] = jnp.concatenate([vtb[h * HEAD_DIM:(h + 1) * HEAD_DIM, rows], pad], axis=0)


def _shared_kv(x, w_k, w_v):
    s, d = x.shape
    aw = w_k.shape[1]
    nblk = s // MOBA_BLOCK
    tm = min(2 * MOBA_BLOCK, s)
    r = tm // MOBA_BLOCK
    npair = aw // PAIR_W
    k4, vt4, km = pl.pallas_call(
        _kv_kernel,
        grid=(s // tm,),
        in_specs=[pl.BlockSpec((tm, d), lambda i: (i, 0)), _const_spec((d, aw)), _const_spec((aw, d))],
        out_specs=[pl.BlockSpec((npair, r, MOBA_BLOCK, PAIR_W), lambda i: (0, i, 0, 0)),
                   pl.BlockSpec((N_HEADS, r, VT_ROWS, MOBA_BLOCK), lambda i: (0, i, 0, 0)),
                   pl.BlockSpec((r, 1, aw), lambda i: (i, 0, 0))],
        out_shape=[jax.ShapeDtypeStruct((npair, nblk, MOBA_BLOCK, PAIR_W), BF16),
                   jax.ShapeDtypeStruct((N_HEADS, nblk, VT_ROWS, MOBA_BLOCK), BF16),
                   jax.ShapeDtypeStruct((nblk, 1, aw), F32)],
        compiler_params=_params(),
        name="shared_kv",
    )(x, w_k.astype(BF16), w_v.T.astype(BF16))
    return k4, vt4, km.reshape(nblk, aw).astype(BF16)


def _t5_bucket_np(dist):
    n = np.maximum(dist, 0)
    max_exact = REL_BUCKETS // 2
    nf = np.maximum(n, 1).astype(np.float32)
    large = max_exact + (np.log(nf / np.float32(max_exact)) / np.float32(math.log(REL_MAX_DIST / max_exact))
                         * np.float32(REL_BUCKETS - max_exact)).astype(np.int32)
    large = np.minimum(large, REL_BUCKETS - 1)
    return np.where(n < max_exact, n, large).astype(np.int32)


def _bucket_tiles(seq):
    kpos = np.arange(MOBA_BLOCK)[:, None]
    qpos = np.arange(MOBA_BLOCK)[None, :]
    own = _t5_bucket_np(qpos - kpos)
    prev = _t5_bucket_np(qpos - kpos + MOBA_BLOCK)
    far = _t5_bucket_np(np.arange(MOBA_BLOCK + 1, max(seq, MOBA_BLOCK + 2)))
    assert (far == far[0]).all(), "bias must be constant beyond the previous block"
    return own, prev, int(far[0])


def _q_kernel(x_ref, wq_ref, qg_ref, qs_ref):
    q = _dot(x_ref[...].astype(BF16), wq_ref[...]) * np.float32(HEAD_DIM ** -0.5)
    qg_ref[...] = q.astype(BF16)
    qs_ref[...] = (q * np.float32(LOG2E)).astype(BF16)


def _q_proj(x, w_q):
    s, d = x.shape
    aw = w_q.shape[1]
    tm = min(MIX_ROWS, s)
    return pl.pallas_call(
        _q_kernel,
        grid=(s // tm,),
        in_specs=[pl.BlockSpec((tm, d), lambda i: (i, 0)), _const_spec((d, aw))],
        out_specs=[pl.BlockSpec((tm, aw), lambda i: (i, 0))] * 2,
        out_shape=[jax.ShapeDtypeStruct((s, aw), BF16)] * 2,
        compiler_params=_params(),
        name="q_proj",
    )(x, w_q.astype(BF16))


def _moba_kernel(rb_ref, qg_ref, qs_ref, k_ref, vt_ref, km_ref, bo_ref, bp_ref, o_ref, bias_scr, sel_scr,
                 s_scr, *, far_bucket):
    hp = pl.program_id(0)
    i = pl.program_id(1)
    tq = qg_ref.shape[0]
    nblk = km_ref.shape[0]
    heads = range(HEADS_PER_STEP)

    @pl.when(i == 0)
    def _():
        kpos = lax.broadcasted_iota(jnp.int32, (MOBA_BLOCK, tq), 0)
        qpos = lax.broadcasted_iota(jnp.int32, (MOBA_BLOCK, tq), 1)
        for hh in heads:
            h = hp * HEADS_PER_STEP + hh
            for kind, bkt_ref in ((KIND_OWN, bo_ref), (KIND_PREV, bp_ref)):
                bkt = bkt_ref[...]
                tile = jnp.zeros(bkt.shape, F32)
                for bucket in range(REL_BUCKETS):
                    tile = jnp.where(bkt == bucket, rb_ref[h, bucket], tile)
                tile = tile * np.float32(LOG2E)
                if kind == KIND_OWN:
                    tile = jnp.where(qpos >= kpos, tile, NEG_BIG)
                bias_scr[hh, kind] = tile
            bias_scr[hh, KIND_FAR] = jnp.full((MOBA_BLOCK, tq), rb_ref[h, far_bucket] * np.float32(LOG2E), F32)

    lane = lax.broadcasted_iota(jnp.int32, (tq, PAIR_W), 1)
    blk_row = lax.broadcasted_iota(jnp.int32, (nblk, tq), 0)

    def pair_cols(hh):
        pair = hh // PAIR_HEADS
        return slice(pair * PAIR_W, (pair + 1) * PAIR_W)

    def head_only(q_ref, hh):
        lo = (hh % PAIR_HEADS) * HEAD_DIM
        mine = (lane >= lo) & (lane < lo + HEAD_DIM)
        return jnp.where(mine, q_ref[:, pair_cols(hh)].astype(F32), 0.0).astype(BF16)

    qss = [head_only(qs_ref, hh) for hh in heads]

    def scores(step, slot):
        j0 = jnp.minimum(STEP_BLOCKS * step, nblk - STEP_BLOCKS)
        for pair in range(PAIRS_PER_STEP):
            ks = k_ref[pair, pl.ds(j0, STEP_BLOCKS)].reshape(STEP_BLOCKS * MOBA_BLOCK, PAIR_W)
            for hh in range(pair * PAIR_HEADS, (pair + 1) * PAIR_HEADS):
                s_scr[slot, hh] = lax.dot_general(ks, qss[hh], _NT, preferred_element_type=F32)

    gates = [lax.dot_general(km_ref[:, pair_cols(hh)], head_only(qg_ref, hh), _NT, preferred_element_type=F32)
             for hh in heads]
    scores(0, 0)
    for hh in heads:
        gate = jnp.where(blk_row < i, gates[hh], -jnp.inf)
        sel = jnp.where(blk_row == i, 1.0, 0.0)
        for _ in range(MOBA_TOPK):
            top = jnp.max(gate, axis=0, keepdims=True)
            first = jnp.min(jnp.where(gate == top, blk_row, nblk), axis=0, keepdims=True)
            pick = (blk_row == first) & (top > -jnp.inf)
            sel = jnp.where(pick, 1.0, sel)
            gate = jnp.where(pick, -jnp.inf, gate)
        sel_scr[hh] = sel

    c_far = [rb_ref[hp * HEADS_PER_STEP + hh, far_bucket] * np.float32(LOG2E) for hh in heads]

    def consume(step, slot, carries, far_only):
        new = []
        for hh in heads:
            m, acc = carries[hh]
            xs, ons, vts = [], [], []
            shift = c_far[hh] if far_only else 0.0
            m_new = m
            for t in range(STEP_BLOCKS):
                j = STEP_BLOCKS * step + t
                x = s_scr[slot, hh, t * MOBA_BLOCK:(t + 1) * MOBA_BLOCK]
                if far_only:
                    jc = j
                    on = sel_scr[hh, pl.ds(jc, 1), :] > 0.0
                else:
                    jc = jnp.minimum(j, nblk - 1)
                    kind = jnp.where(j == i, KIND_OWN, jnp.where(j == i - 1, KIND_PREV, KIND_FAR))
                    x = x + bias_scr[hh, kind]
                    on = (sel_scr[hh, pl.ds(jc, 1), :] > 0.0) & (j <= i)
                m_new = jnp.maximum(m_new, jnp.where(on, jnp.max(x, axis=0, keepdims=True) + shift, NEG_BIG))
                xs.append(x), ons.append(on), vts.append(vt_ref[hh, jc])
            ps = [jnp.exp2(x - jnp.where(on, m_new - shift, POS_BIG)).astype(BF16) for x, on in zip(xs, ons)]
            pv = _dot(jnp.concatenate(vts, axis=1), jnp.concatenate(ps, axis=0))
            new.append((m_new, acc * jnp.exp2(m - m_new) + pv))
        return tuple(new)

    def trip(t, carries, first_step, steps, far_only):
        base = first_step + steps * t
        for u in range(steps):
            scores(base + u + 1, (u + 1) % 2)
            carries = consume(base + u, u % 2, carries, far_only)
        return carries

    def run(n, carries, **kw):
        return lax.fori_loop(0, n, functools.partial(trip, **kw), carries)

    n_far = jnp.maximum(i - 1, 0) // (2 * STEP_BLOCKS) * 2
    init = (jnp.full((1, tq), NEG_BIG, F32), jnp.zeros((VT_ROWS, tq), F32))
    carries = (init,) * HEADS_PER_STEP
    done = 0
    for steps in TRIP_STEPS:
        n = (n_far - done) // steps
        carries = run(n, carries, first_step=done, steps=steps, far_only=True)
        done = done + n * steps
    n_last = i // STEP_BLOCKS + 1 - n_far
    scores(n_far + 1, 1)
    carries = consume(n_far, 0, carries, False)

    def second(c):
        scores(n_far + 2, 0)
        return consume(n_far + 1, 1, c, False)

    carries = lax.cond(n_last >= 2, second, lambda c: c, carries)
    carries = lax.cond(n_last >= 3, lambda c: consume(n_far + 2, 0, c, False), lambda c: c, carries)
    outs = [acc[:HEAD_DIM] / acc[HEAD_DIM:HEAD_DIM + 1] for _, acc in carries]
    o_ref[...] = jnp.concatenate(outs, axis=0).T.astype(o_ref.dtype)


def _moba_attention(q_gate, q_score, k4, vt4, km, rel_bias):
    s, aw = q_gate.shape
    nblk = s // MOBA_BLOCK
    own, prev, far_bucket = _bucket_tiles(s)
    assert nblk % STEP_BLOCKS == 0 and aw % STEP_W == 0
    tile = pl.BlockSpec((MOBA_BLOCK, STEP_W), lambda hp, i: (i, hp))
    once = dict(pipeline_mode=pl.Buffered(1))
    return pl.pallas_call(
        functools.partial(_moba_kernel, far_bucket=far_bucket),
        grid=(aw // STEP_W, nblk),
        in_specs=[pl.BlockSpec(memory_space=pltpu.SMEM),
                  tile, tile,
                  pl.BlockSpec((PAIRS_PER_STEP, nblk, MOBA_BLOCK, PAIR_W), lambda hp, i: (hp, 0, 0, 0), **once),
                  pl.BlockSpec((HEADS_PER_STEP, nblk, VT_ROWS, MOBA_BLOCK), lambda hp, i: (hp, 0, 0, 0), **once),
                  pl.BlockSpec((nblk, STEP_W), lambda hp, i: (0, hp)),
                  pl.BlockSpec((MOBA_BLOCK, MOBA_BLOCK), lambda hp, i: (0, 0)),
                  pl.BlockSpec((MOBA_BLOCK, MOBA_BLOCK), lambda hp, i: (0, 0))],
        out_specs=tile,
        out_shape=jax.ShapeDtypeStruct((s, aw), BF16),
        scratch_shapes=[pltpu.VMEM((HEADS_PER_STEP, 3, MOBA_BLOCK, MOBA_BLOCK), F32),
                        pltpu.VMEM((HEADS_PER_STEP, nblk, MOBA_BLOCK), F32),
                        pltpu.VMEM((2, HEADS_PER_STEP, STEP_BLOCKS * MOBA_BLOCK, MOBA_BLOCK), F32)],
        compiler_params=_params(2),
        name="moba_attn",
    )(rel_bias, q_gate, q_score, k4, vt4, km, jnp.asarray(own), jnp.asarray(prev))


def _proj_kernel(a_ref, x_ref, w_ref, g_ref, b_ref, o_ref):
    y = DEEPNORM_ALPHA * x_ref[...] + _dot(a_ref[...], w_ref[...])
    o_ref[...] = _layer_norm(y, g_ref[...], b_ref[...])


def _proj_ln(a, x, w, g, b):
    s, d = x.shape
    aw = a.shape[1]
    tm = min(MIX_ROWS, s)
    return pl.pallas_call(
        _proj_kernel,
        grid=(s // tm,),
        in_specs=[pl.BlockSpec((tm, aw), lambda i: (i, 0)), pl.BlockSpec((tm, d), lambda i: (i, 0)),
                  _const_spec((aw, d)), _const_spec((1, d)), _const_spec((1, d))],
        out_specs=pl.BlockSpec((tm, d), lambda i: (i, 0)),
        out_shape=jax.ShapeDtypeStruct((s, d), F32),
        compiler_params=_params(),
        name="attn_out_ln",
    )(a, x, w.astype(BF16), g.reshape(1, d), b.reshape(1, d))


def kernel(x, ln_g, ln_b, ffn_w_gate, ffn_w_up, ffn_w_down, gm_w_in, gm_sgu_g, gm_sgu_b, gm_w_s, gm_b_s, gm_w_out, attn_w_q, attn_w_o, w_k_shared, w_v_shared, rel_bias):
    batch, seq, d = x.shape
    assert seq % MOBA_BLOCK == 0 and w_k_shared.shape[1] == N_HEADS * HEAD_DIM
    rows = x.reshape(batch * seq, d)
    ffn_w = (ffn_w_gate.astype(BF16), ffn_w_up.astype(BF16), ffn_w_down.astype(BF16))
    outs = []
    for bi in range(batch):
        h = rows[bi * seq:(bi + 1) * seq]
        kv = None
        for layer in range(DEPTH):
            h = _ffn_ln(h, *ffn_w, layer, 0, ln_g[layer, 0], ln_b[layer, 0])
            if layer < N_A_LAYERS:
                a = layer
                h = _gmlp_ln(h, gm_w_in[a], gm_sgu_g[a], gm_sgu_b[a], gm_w_s[a], gm_b_s[a], gm_w_out[a],
                             ln_g[layer, 1], ln_b[layer, 1])
            else:
                j = layer - N_A_LAYERS
                o = _moba_attention(*_q_proj(h, attn_w_q[j]), *kv, rel_bias)
                h = _proj_ln(o, h, attn_w_o[j], ln_g[layer, 1], ln_b[layer, 1])
            h = _ffn_ln(h, *ffn_w, layer, 1, ln_g[layer, 2], ln_b[layer, 2])
            if layer == N_A_LAYERS - 1:
                kv = _shared_kv(h, w_k_shared, w_v_shared)
        outs.append(h)
    return jnp.concatenate(outs, axis=0).reshape(batch, seq, d)
```

```python
import functools
import math

import jax
import jax.numpy as jnp
import numpy as np
from jax import lax
from jax.experimental import pallas as pl
from jax.experimental.pallas import tpu as pltpu

DEPTH = 4
N_A_LAYERS = DEPTH // 2
GM_GROUPS = 8
GM_CHUNK = 128
N_HEADS = 16
HEAD_DIM = 64
MOBA_BLOCK = 256
MOBA_TOPK = 3
REL_BUCKETS = 32
REL_MAX_DIST = 128
DEEPNORM_ALPHA = (2 * DEPTH) ** 0.25
LN_EPS = 1e-5

PAIR_HEADS = 2
PAIR_W = PAIR_HEADS * HEAD_DIM
PAIRS_PER_STEP = 2
HEADS_PER_STEP = PAIRS_PER_STEP * PAIR_HEADS
STEP_W = PAIRS_PER_STEP * PAIR_W
VT_ROWS = HEAD_DIM + 16
STEP_BLOCKS = 1
TRIP_STEPS = (16, 8, 4, 2)
KIND_OWN, KIND_PREV, KIND_FAR = 0, 1, 2
LOG2E = math.log2(math.e)
NEG_BIG = -1e30
POS_BIG = 1e30

FFN_ROWS = 512
FFN_CHUNK = 256
MIX_ROWS = 512
VMEM_LIMIT = 56 * 1024 * 1024

F32 = jnp.float32
BF16 = jnp.bfloat16
_NT = (((1,), (1,)), ((), ()))


def _params(n_axes=1):
    return pltpu.CompilerParams(dimension_semantics=("arbitrary",) * n_axes,
                                vmem_limit_bytes=VMEM_LIMIT)


def _const_spec(shape):
    zeros = (0,) * len(shape)
    return pl.BlockSpec(shape, lambda *_: zeros, pipeline_mode=pl.Buffered(1))


def _layer_norm(y, g, b):
    mu = jnp.mean(y, axis=-1, keepdims=True)
    d = y - mu
    var = jnp.mean(d * d, axis=-1, keepdims=True)
    return d * lax.rsqrt(var + LN_EPS) * g + b


def _dot(a, b):
    return jnp.dot(a, b, preferred_element_type=F32)


def _ffn_kernel(x_ref, wg_ref, wu_ref, wd_ref, g_ref, b_ref, o_ref):
    half = x_ref.shape[0] // 2
    ys = []
    for r in range(2):
        x = x_ref[r * half:(r + 1) * half]
        xb = x.astype(BF16)
        h1 = _dot(xb, wg_ref[...])
        h2 = _dot(xb, wu_ref[...])
        a = ((h1 * jax.nn.sigmoid(h1)) * h2).astype(BF16)
        ys.append(DEEPNORM_ALPHA * x + 0.5 * _dot(a, wd_ref[...]))
    for r in range(2):
        o_ref[r * half:(r + 1) * half] = _layer_norm(ys[r], g_ref[...], b_ref[...])


def _ffn_ln(x, w_gate, w_up, w_down, layer, which, g, b):
    s, d = x.shape
    f = w_gate.shape[-1]
    tm = min(FFN_ROWS, s)
    row = pl.BlockSpec((tm, d), lambda i: (i, 0))

    def pick(*shape):
        return pl.BlockSpec((None, None) + shape, lambda i: (layer, which, 0, 0), pipeline_mode=pl.Buffered(1))

    return pl.pallas_call(
        _ffn_kernel,
        grid=(s // tm,),
        in_specs=[row, pick(d, f), pick(d, f), pick(f, d), _const_spec((1, d)), _const_spec((1, d))],
        out_specs=row,
        out_shape=jax.ShapeDtypeStruct((s, d), F32),
        compiler_params=_params(),
        name="ffn_ln",
    )(x, w_gate, w_up, w_down, g.reshape(1, d), b.reshape(1, d))


def _gmlp_kernel(x_ref, win_ref, sg_ref, sb_ref, ws_ref, bs_ref, wout_ref, g_ref, b_ref, o_ref,
                 v_scr, mix_scr):
    tm = x_ref.shape[0]
    w = wout_ref.shape[0]
    gd = w // GM_GROUPS
    nch = tm // GM_CHUNK
    x = x_ref[...]
    z = _dot(x.astype(BF16), win_ref[...])
    z = 0.5 * z * (1.0 + lax.erf(z * np.float32(math.sqrt(0.5))))
    u = z[:, :w]
    v_scr[...] = _layer_norm(z[:, w:], sg_ref[...], sb_ref[...]).astype(BF16)
    t_idx = lax.broadcasted_iota(jnp.int32, (GM_CHUNK, GM_CHUNK), 0)
    s_idx = lax.broadcasted_iota(jnp.int32, (GM_CHUNK, GM_CHUNK), 1)
    for grp in range(GM_GROUPS):
        cols = slice(grp * gd, (grp + 1) * gd)
        w_tril = jnp.where(t_idx >= s_idx, ws_ref[grp], 0.0).astype(BF16)
        vg = jnp.concatenate([v_scr[c * GM_CHUNK:(c + 1) * GM_CHUNK, cols] for c in range(nch)], axis=1)
        mg = _dot(w_tril, vg)
        for c in range(nch):
            mix_scr[c * GM_CHUNK:(c + 1) * GM_CHUNK, cols] = mg[:, c * gd:(c + 1) * gd] + bs_ref[:, cols]
    y = (u * mix_scr[...]).astype(BF16)
    out = DEEPNORM_ALPHA * x + _dot(y, wout_ref[...])
    o_ref[...] = _layer_norm(out, g_ref[...], b_ref[...])


def _gmlp_ln(x, w_in, sgu_g, sgu_b, w_s, b_s, w_out, g, b):
    s, d = x.shape
    w = w_out.shape[0]
    tm = min(MIX_ROWS, s)
    bs_full = jnp.repeat(b_s.T, w // GM_GROUPS, axis=1)
    row = pl.BlockSpec((tm, d), lambda i: (i, 0))
    return pl.pallas_call(
        _gmlp_kernel,
        grid=(s // tm,),
        in_specs=[row, _const_spec((d, 2 * w)), _const_spec((1, w)), _const_spec((1, w)),
                  _const_spec((GM_GROUPS, GM_CHUNK, GM_CHUNK)), _const_spec((GM_CHUNK, w)),
                  _const_spec((w, d)), _const_spec((1, d)), _const_spec((1, d))],
        out_specs=row,
        out_shape=jax.ShapeDtypeStruct((s, d), F32),
        scratch_shapes=[pltpu.VMEM((tm, w), BF16), pltpu.VMEM((tm, w), F32)],
        compiler_params=_params(),
        name="gmlp_ln",
    )(x, w_in.astype(BF16), sgu_g.reshape(1, w), sgu_b.reshape(1, w), w_s, bs_full,
      w_out.astype(BF16), g.reshape(1, d), b.reshape(1, d))


def _kv_kernel(x_ref, wk_ref, wvt_ref, k_ref, vt_ref, km_ref):
    tm = x_ref.shape[0]
    xb = x_ref[...].astype(BF16)
    k = _dot(xb, wk_ref[...])
    vt = lax.dot_general(wvt_ref[...], xb, _NT, preferred_element_type=F32)
    kb = k.astype(BF16)
    vtb = vt.astype(BF16)
    pad_row = lax.broadcasted_iota(jnp.int32, (VT_ROWS - HEAD_DIM, MOBA_BLOCK), 0)
    pad = jnp.where(pad_row == 0, 1.0, 0.0).astype(BF16)
    for blk in range(tm // MOBA_BLOCK):
        rows = slice(blk * MOBA_BLOCK, (blk + 1) * MOBA_BLOCK)
        km_ref[blk] = jnp.mean(k[rows], axis=0, keepdims=True)
        for p in range(k_ref.shape[0]):
            k_ref[p, blk] = kb[rows, p * PAIR_W:(p + 1) * PAIR_W]
        for h in range(vt_ref.shape[0]):
            vt_ref[h, blk] = jnp.concatenate([vtb[h * HEAD_DIM:(h + 1) * HEAD_DIM, rows], pad], axis=0)


def _shared_kv(x, w_k, w_v):
    s, d = x.shape
    aw = w_k.shape[1]
    nblk = s // MOBA_BLOCK
    tm = min(2 * MOBA_BLOCK, s)
    r = tm // MOBA_BLOCK
    npair = aw // PAIR_W
    k4, vt4, km = pl.pallas_call(
        _kv_kernel,
        grid=(s // tm,),
        in_specs=[pl.BlockSpec((tm, d), lambda i: (i, 0)), _const_spec((d, aw)), _const_spec((aw, d))],
        out_specs=[pl.BlockSpec((npair, r, MOBA_BLOCK, PAIR_W), lambda i: (0, i, 0, 0)),
                   pl.BlockSpec((N_HEADS, r, VT_ROWS, MOBA_BLOCK), lambda i: (0, i, 0, 0)),
                   pl.BlockSpec((r, 1, aw), lambda i: (i, 0, 0))],
        out_shape=[jax.ShapeDtypeStruct((npair, nblk, MOBA_BLOCK, PAIR_W), BF16),
                   jax.ShapeDtypeStruct((N_HEADS, nblk, VT_ROWS, MOBA_BLOCK), BF16),
                   jax.ShapeDtypeStruct((nblk, 1, aw), F32)],
        compiler_params=_params(),
        name="shared_kv",
    )(x, w_k.astype(BF16), w_v.T.astype(BF16))
    return k4, vt4, km.reshape(nblk, aw).astype(BF16)


def _t5_bucket_np(dist):
    n = np.maximum(dist, 0)
    max_exact = REL_BUCKETS // 2
    nf = np.maximum(n, 1).astype(np.float32)
    large = max_exact + (np.log(nf / np.float32(max_exact)) / np.float32(math.log(REL_MAX_DIST / max_exact))
                         * np.float32(REL_BUCKETS - max_exact)).astype(np.int32)
    large = np.minimum(large, REL_BUCKETS - 1)
    return np.where(n < max_exact, n, large).astype(np.int32)


def _bucket_tiles(seq):
    kpos = np.arange(MOBA_BLOCK)[:, None]
    qpos = np.arange(MOBA_BLOCK)[None, :]
    own = _t5_bucket_np(qpos - kpos)
    prev = _t5_bucket_np(qpos - kpos + MOBA_BLOCK)
    far = _t5_bucket_np(np.arange(MOBA_BLOCK + 1, max(seq, MOBA_BLOCK + 2)))
    assert (far == far[0]).all(), "bias must be constant beyond the previous block"
    return own, prev, int(far[0])


def _q_kernel(x_ref, wq_ref, qg_ref, qs_ref):
    q = _dot(x_ref[...].astype(BF16), wq_ref[...]) * np.float32(HEAD_DIM ** -0.5)
    qg_ref[...] = q.astype(BF16)
    qs_ref[...] = (q * np.float32(LOG2E)).astype(BF16)


def _q_proj(x, w_q):
    s, d = x.shape
    aw = w_q.shape[1]
    tm = min(MIX_ROWS, s)
    return pl.pallas_call(
        _q_kernel,
        grid=(s // tm,),
        in_specs=[pl.BlockSpec((tm, d), lambda i: (i, 0)), _const_spec((d, aw))],
        out_specs=[pl.BlockSpec((tm, aw), lambda i: (i, 0))] * 2,
        out_shape=[jax.ShapeDtypeStruct((s, aw), BF16)] * 2,
        compiler_params=_params(),
        name="q_proj",
    )(x, w_q.astype(BF16))


def _moba_kernel(rb_ref, qg_ref, qs_ref, k_ref, vt_ref, km_ref, bo_ref, bp_ref, o_ref, bias_scr, sel_scr,
                 s_scr, *, far_bucket):
    hp = pl.program_id(0)
    i = pl.program_id(1)
    tq = qg_ref.shape[0]
    nblk = km_ref.shape[0]
    heads = range(HEADS_PER_STEP)

    @pl.when(i == 0)
    def _():
        kpos = lax.broadcasted_iota(jnp.int32, (MOBA_BLOCK, tq), 0)
        qpos = lax.broadcasted_iota(jnp.int32, (MOBA_BLOCK, tq), 1)
        for hh in heads:
            h = hp * HEADS_PER_STEP + hh
            for kind, bkt_ref in ((KIND_OWN, bo_ref), (KIND_PREV, bp_ref)):
                bkt = bkt_ref[...]
                tile = jnp.zeros(bkt.shape, F32)
                for bucket in range(REL_BUCKETS):
                    tile = jnp.where(bkt == bucket, rb_ref[h, bucket], tile)
                tile = tile * np.float32(LOG2E)
                if kind == KIND_OWN:
                    tile = jnp.where(qpos >= kpos, tile, NEG_BIG)
                bias_scr[hh, kind] = tile
            bias_scr[hh, KIND_FAR] = jnp.full((MOBA_BLOCK, tq), rb_ref[h, far_bucket] * np.float32(LOG2E), F32)

    lane = lax.broadcasted_iota(jnp.int32, (tq, PAIR_W), 1)
    blk_row = lax.broadcasted_iota(jnp.int32, (nblk, tq), 0)

    def pair_cols(hh):
        pair = hh // PAIR_HEADS
        return slice(pair * PAIR_W, (pair + 1) * PAIR_W)

    def head_only(q_ref, hh):
        lo = (hh % PAIR_HEADS) * HEAD_DIM
        mine = (lane >= lo) & (lane < lo + HEAD_DIM)
        return jnp.where(mine, q_ref[:, pair_cols(hh)].astype(F32), 0.0).astype(BF16)

    qss = [head_only(qs_ref, hh) for hh in heads]

    def scores(step, slot):
        j0 = jnp.minimum(STEP_BLOCKS * step, nblk - STEP_BLOCKS)
        for pair in range(PAIRS_PER_STEP):
            ks = k_ref[pair, pl.ds(j0, STEP_BLOCKS)].reshape(STEP_BLOCKS * MOBA_BLOCK, PAIR_W)
            for hh in range(pair * PAIR_HEADS, (pair + 1) * PAIR_HEADS):
                s_scr[slot, hh] = lax.dot_general(ks, qss[hh], _NT, preferred_element_type=F32)

    gates = [lax.dot_general(km_ref[:, pair_cols(hh)], head_only(qg_ref, hh), _NT, preferred_element_type=F32)
             for hh in heads]
    scores(0, 0)
    for hh in heads:
        gate = jnp.where(blk_row < i, gates[hh], -jnp.inf)
        sel = jnp.where(blk_row == i, 1.0, 0.0)
        for _ in range(MOBA_TOPK):
            top = jnp.max(gate, axis=0, keepdims=True)
            first = jnp.min(jnp.where(gate == top, blk_row, nblk), axis=0, keepdims=True)
            pick = (blk_row == first) & (top > -jnp.inf)
            sel = jnp.where(pick, 1.0, sel)
            gate = jnp.where(pick, -jnp.inf, gate)
        sel_scr[hh] = sel

    c_far = [rb_ref[hp * HEADS_PER_STEP + hh, far_bucket] * np.float32(LOG2E) for hh in heads]

    def consume(step, slot, carries, far_only):
        new = []
        for hh in heads:
            m, acc = carries[hh]
            xs, ons, vts = [], [], []
            shift = c_far[hh] if far_only else 0.0
            m_new = m
            for t in range(STEP_BLOCKS):
                j = STEP_BLOCKS * step + t
                x = s_scr[slot, hh, t * MOBA_BLOCK:(t + 1) * MOBA_BLOCK]
                if far_only:
                    jc = j
                    on = sel_scr[hh, pl.ds(jc, 1), :] > 0.0
                else:
                    jc = jnp.minimum(j, nblk - 1)
                    kind = jnp.where(j == i, KIND_OWN, jnp.where(j == i - 1, KIND_PREV, KIND_FAR))
                    x = x + bias_scr[hh, kind]
                    on = (sel_scr[hh, pl.ds(jc, 1), :] > 0.0) & (j <= i)
                m_new = jnp.maximum(m_new, jnp.where(on, jnp.max(x, axis=0, keepdims=True) + shift, NEG_BIG))
                xs.append(x), ons.append(on), vts.append(vt_ref[hh, jc])
            ps = [jnp.exp2(x - jnp.where(on, m_new - shift, POS_BIG)).astype(BF16) for x, on in zip(xs, ons)]
            pv = _dot(jnp.concatenate(vts, axis=1), jnp.concatenate(ps, axis=0))
            new.append((m_new, acc * jnp.exp2(m - m_new) + pv))
        return tuple(new)

    def trip(t, carries, first_step, steps, far_only):
        base = first_step + steps * t
        for u in range(steps):
            scores(base + u + 1, (u + 1) % 2)
            carries = consume(base + u, u % 2, carries, far_only)
        return carries

    def run(n, carries, **kw):
        return lax.fori_loop(0, n, functools.partial(trip, **kw), carries)

    n_far = jnp.maximum(i - 1, 0) // (2 * STEP_BLOCKS) * 2
    init = (jnp.full((1, tq), NEG_BIG, F32), jnp.zeros((VT_ROWS, tq), F32))
    carries = (init,) * HEADS_PER_STEP
    done = 0
    for steps in TRIP_STEPS:
        n = (n_far - done) // steps
        carries = run(n, carries, first_step=done, steps=steps, far_only=True)
        done = done + n * steps
    n_last = i // STEP_BLOCKS + 1 - n_far
    scores(n_far + 1, 1)
    carries = consume(n_far, 0, carries, False)

    def second(c):
        scores(n_far + 2, 0)
        return consume(n_far + 1, 1, c, False)

    carries = lax.cond(n_last >= 2, second, lambda c: c, carries)
    carries = lax.cond(n_last >= 3, lambda c: consume(n_far + 2, 0, c, False), lambda c: c, carries)
    outs = [acc[:HEAD_DIM] / acc[HEAD_DIM:HEAD_DIM + 1] for _, acc in carries]
    o_ref[...] = jnp.concatenate(outs, axis=0).T.astype(o_ref.dtype)


def _moba_attention(q_gate, q_score, k4, vt4, km, rel_bias):
    s, aw = q_gate.shape
    nblk = s // MOBA_BLOCK
    own, prev, far_bucket = _bucket_tiles(s)
    assert nblk % STEP_BLOCKS == 0 and aw % STEP_W == 0
    tile = pl.BlockSpec((MOBA_BLOCK, STEP_W), lambda hp, i: (i, hp))
    once = dict(pipeline_mode=pl.Buffered(1))
    return pl.pallas_call(
        functools.partial(_moba_kernel, far_bucket=far_bucket),
        grid=(aw // STEP_W, nblk),
        in_specs=[pl.BlockSpec(memory_space=pltpu.SMEM),
                  tile, tile,
                  pl.BlockSpec((PAIRS_PER_STEP, nblk, MOBA_BLOCK, PAIR_W), lambda hp, i: (hp, 0, 0, 0), **once),
                  pl.BlockSpec((HEADS_PER_STEP, nblk, VT_ROWS, MOBA_BLOCK), lambda hp, i: (hp, 0, 0, 0), **once),
                  pl.BlockSpec((nblk, STEP_W), lambda hp, i: (0, hp)),
                  pl.BlockSpec((MOBA_BLOCK, MOBA_BLOCK), lambda hp, i: (0, 0)),
                  pl.BlockSpec((MOBA_BLOCK, MOBA_BLOCK), lambda hp, i: (0, 0))],
        out_specs=tile,
        out_shape=jax.ShapeDtypeStruct((s, aw), BF16),
        scratch_shapes=[pltpu.VMEM((HEADS_PER_STEP, 3, MOBA_BLOCK, MOBA_BLOCK), F32),
                        pltpu.VMEM((HEADS_PER_STEP, nblk, MOBA_BLOCK), F32),
                        pltpu.VMEM((2, HEADS_PER_STEP, STEP_BLOCKS * MOBA_BLOCK, MOBA_BLOCK), F32)],
        compiler_params=_params(2),
        name="moba_attn",
    )(rel_bias, q_gate, q_score, k4, vt4, km, jnp.asarray(own), jnp.asarray(prev))


def _proj_kernel(a_ref, x_ref, w_ref, g_ref, b_ref, o_ref):
    y = DEEPNORM_ALPHA * x_ref[...] + _dot(a_ref[...], w_ref[...])
    o_ref[...] = _layer_norm(y, g_ref[...], b_ref[...])


def _proj_ln(a, x, w, g, b):
    s, d = x.shape
    aw = a.shape[1]
    tm = min(MIX_ROWS, s)
    return pl.pallas_call(
        _proj_kernel,
        grid=(s // tm,),
        in_specs=[pl.BlockSpec((tm, aw), lambda i: (i, 0)), pl.BlockSpec((tm, d), lambda i: (i, 0)),
                  _const_spec((aw, d)), _const_spec((1, d)), _const_spec((1, d))],
        out_specs=pl.BlockSpec((tm, d), lambda i: (i, 0)),
        out_shape=jax.ShapeDtypeStruct((s, d), F32),
        compiler_params=_params(),
        name="attn_out_ln",
    )(a, x, w.astype(BF16), g.reshape(1, d), b.reshape(1, d))


def kernel(x, ln_g, ln_b, ffn_w_gate, ffn_w_up, ffn_w_down, gm_w_in, gm_sgu_g, gm_sgu_b, gm_w_s, gm_b_s, gm_w_out, attn_w_q, attn_w_o, w_k_shared, w_v_shared, rel_bias):
    batch, seq, d = x.shape
    assert seq % MOBA_BLOCK == 0 and w_k_shared.shape[1] == N_HEADS * HEAD_DIM
    rows = x.reshape(batch * seq, d)
    ffn_w = (ffn_w_gate.astype(BF16), ffn_w_up.astype(BF16), ffn_w_down.astype(BF16))
    outs = []
    for bi in range(batch):
        h = rows[bi * seq:(bi + 1) * seq]
        kv = None
        for layer in range(DEPTH):
            h = _ffn_ln(h, *ffn_w, layer, 0, ln_g[layer, 0], ln_b[layer, 0])
            if layer < N_A_LAYERS:
                a = layer
                h = _gmlp_ln(h, gm_w_in[a], gm_sgu_g[a], gm_sgu_b[a], gm_w_s[a], gm_b_s[a], gm_w_out[a],
                             ln_g[layer, 1], ln_b[layer, 1])
            else:
                j = layer - N_A_LAYERS
                o = _moba_attention(*_q_proj(h, attn_w_q[j]), *kv, rel_bias)
                h = _proj_ln(o, h, attn_w_o[j], ln_g[layer, 1], ln_b[layer, 1])
            h = _ffn_ln(h, *ffn_w, layer, 1, ln_g[layer, 2], ln_b[layer, 2])
            if layer == N_A_LAYERS - 1:
                kv = _shared_kv(h, w_k_shared, w_v_shared)
        outs.append(h)
    return jnp.concatenate(outs, axis=0).reshape(batch, seq, d)
```

```python
import functools
import math

import jax
import jax.numpy as jnp
import numpy as np
from jax import lax
from jax.experimental import pallas as pl
from jax.experimental.pallas import tpu as pltpu

DEPTH = 4
N_A_LAYERS = DEPTH // 2
GM_GROUPS = 8
GM_CHUNK = 128
N_HEADS = 16
HEAD_DIM = 64
MOBA_BLOCK = 256
MOBA_TOPK = 3
REL_BUCKETS = 32
REL_MAX_DIST = 128
DEEPNORM_ALPHA = (2 * DEPTH) ** 0.25
LN_EPS = 1e-5

PAIR_HEADS = 2
PAIR_W = PAIR_HEADS * HEAD_DIM
PAIRS_PER_STEP = 2
HEADS_PER_STEP = PAIRS_PER_STEP * PAIR_HEADS
STEP_W = PAIRS_PER_STEP * PAIR_W
VT_ROWS = HEAD_DIM + 16
STEP_BLOCKS = 1
TRIP_STEPS = (16, 8, 4, 2)
KIND_OWN, KIND_PREV, KIND_FAR = 0, 1, 2
LOG2E = math.log2(math.e)
NEG_BIG = -1e30
POS_BIG = 1e30

FFN_ROWS = 1024
FFN_SPLIT = 4
MIX_ROWS = 512
VMEM_LIMIT = 56 * 1024 * 1024

F32 = jnp.float32
BF16 = jnp.bfloat16
_NT = (((1,), (1,)), ((), ()))


def _params(n_axes=1):
    return pltpu.CompilerParams(dimension_semantics=("arbitrary",) * n_axes,
                                vmem_limit_bytes=VMEM_LIMIT)


def _const_spec(shape):
    zeros = (0,) * len(shape)
    return pl.BlockSpec(shape, lambda *_: zeros, pipeline_mode=pl.Buffered(1))


def _layer_norm(y, g, b):
    mu = jnp.mean(y, axis=-1, keepdims=True)
    d = y - mu
    var = jnp.mean(d * d, axis=-1, keepdims=True)
    return d * lax.rsqrt(var + LN_EPS) * g + b


def _dot(a, b):
    return jnp.dot(a, b, preferred_element_type=F32)


def _ffn_kernel(x_ref, wg_ref, wu_ref, wd_ref, g_ref, b_ref, o_ref):
    rows = x_ref.shape[0] // FFN_SPLIT

    def norm_store(r, y):
        o_ref[r * rows:(r + 1) * rows] = _layer_norm(y, g_ref[...], b_ref[...])

    prev = None
    for r in range(FFN_SPLIT):
        x = x_ref[r * rows:(r + 1) * rows]
        xb = x.astype(BF16)
        h1 = _dot(xb, wg_ref[...])
        h2 = _dot(xb, wu_ref[...])
        a = ((h1 * jax.nn.sigmoid(h1)) * h2).astype(BF16)
        y = DEEPNORM_ALPHA * x + 0.5 * _dot(a, wd_ref[...])
        if prev is not None:
            norm_store(r - 1, prev)
        prev = y
    norm_store(FFN_SPLIT - 1, prev)


def _ffn_ln(x, w_gate, w_up, w_down, layer, which, g, b):
    s, d = x.shape
    f = w_gate.shape[-1]
    tm = min(FFN_ROWS, s)
    row = pl.BlockSpec((tm, d), lambda i: (i, 0))

    def pick(*shape):
        return pl.BlockSpec((None, None) + shape, lambda i: (layer, which, 0, 0), pipeline_mode=pl.Buffered(1))

    return pl.pallas_call(
        _ffn_kernel,
        grid=(s // tm,),
        in_specs=[row, pick(d, f), pick(d, f), pick(f, d), _const_spec((1, d)), _const_spec((1, d))],
        out_specs=row,
        out_shape=jax.ShapeDtypeStruct((s, d), F32),
        compiler_params=_params(),
        name="ffn_ln",
    )(x, w_gate, w_up, w_down, g.reshape(1, d), b.reshape(1, d))


def _gmlp_kernel(x_ref, win_ref, sg_ref, sb_ref, ws_ref, bs_ref, wout_ref, g_ref, b_ref, o_ref,
                 v_scr, mix_scr):
    tm = x_ref.shape[0]
    w = wout_ref.shape[0]
    gd = w // GM_GROUPS
    nch = tm // GM_CHUNK
    x = x_ref[...]
    z = _dot(x.astype(BF16), win_ref[...])
    z = 0.5 * z * (1.0 + lax.erf(z * np.float32(math.sqrt(0.5))))
    u = z[:, :w]
    v_scr[...] = _layer_norm(z[:, w:], sg_ref[...], sb_ref[...]).astype(BF16)
    t_idx = lax.broadcasted_iota(jnp.int32, (GM_CHUNK, GM_CHUNK), 0)
    s_idx = lax.broadcasted_iota(jnp.int32, (GM_CHUNK, GM_CHUNK), 1)
    for grp in range(GM_GROUPS):
        cols = slice(grp * gd, (grp + 1) * gd)
        w_tril = jnp.where(t_idx >= s_idx, ws_ref[grp], 0.0).astype(BF16)
        vg = jnp.concatenate([v_scr[c * GM_CHUNK:(c + 1) * GM_CHUNK, cols] for c in range(nch)], axis=1)
        mg = _dot(w_tril, vg)
        for c in range(nch):
            mix_scr[c * GM_CHUNK:(c + 1) * GM_CHUNK, cols] = mg[:, c * gd:(c + 1) * gd] + bs_ref[:, cols]
    y = (u * mix_scr[...]).astype(BF16)
    out = DEEPNORM_ALPHA * x + _dot(y, wout_ref[...])
    o_ref[...] = _layer_norm(out, g_ref[...], b_ref[...])


def _gmlp_ln(x, w_in, sgu_g, sgu_b, w_s, b_s, w_out, g, b):
    s, d = x.shape
    w = w_out.shape[0]
    tm = min(MIX_ROWS, s)
    bs_full = jnp.repeat(b_s.T, w // GM_GROUPS, axis=1)
    row = pl.BlockSpec((tm, d), lambda i: (i, 0))
    return pl.pallas_call(
        _gmlp_kernel,
        grid=(s // tm,),
        in_specs=[row, _const_spec((d, 2 * w)), _const_spec((1, w)), _const_spec((1, w)),
                  _const_spec((GM_GROUPS, GM_CHUNK, GM_CHUNK)), _const_spec((GM_CHUNK, w)),
                  _const_spec((w, d)), _const_spec((1, d)), _const_spec((1, d))],
        out_specs=row,
        out_shape=jax.ShapeDtypeStruct((s, d), F32),
        scratch_shapes=[pltpu.VMEM((tm, w), BF16), pltpu.VMEM((tm, w), F32)],
        compiler_params=_params(),
        name="gmlp_ln",
    )(x, w_in.astype(BF16), sgu_g.reshape(1, w), sgu_b.reshape(1, w), w_s, bs_full,
      w_out.astype(BF16), g.reshape(1, d), b.reshape(1, d))


def _kv_kernel(x_ref, wk_ref, wvt_ref, k_ref, vt_ref, km_ref):
    tm = x_ref.shape[0]
    xb = x_ref[...].astype(BF16)
    k = _dot(xb, wk_ref[...])
    vt = lax.dot_general(wvt_ref[...], xb, _NT, preferred_element_type=F32)
    kb = k.astype(BF16)
    vtb = vt.astype(BF16)
    pad_row = lax.broadcasted_iota(jnp.int32, (VT_ROWS - HEAD_DIM, MOBA_BLOCK), 0)
    pad = jnp.where(pad_row == 0, 1.0, 0.0).astype(BF16)
    for blk in range(tm // MOBA_BLOCK):
        rows = slice(blk * MOBA_BLOCK, (blk + 1) * MOBA_BLOCK)
        km_ref[blk] = jnp.mean(k[rows], axis=0, keepdims=True)
        for p in range(k_ref.shape[0]):
            k_ref[p, blk] = kb[rows, p * PAIR_W:(p + 1) * PAIR_W]
        for h in range(vt_ref.shape[0]):
            vt_ref[h, blk] = jnp.concatenate([vtb[h * HEAD_DIM:(h + 1) * HEAD_DIM, rows], pad], axis=0)


def _shared_kv(x, w_k, w_v):
    s, d = x.shape
    aw = w_k.shape[1]
    nblk = s // MOBA_BLOCK
    tm = min(2 * MOBA_BLOCK, s)
    r = tm // MOBA_BLOCK
    npair = aw // PAIR_W
    k4, vt4, km = pl.pallas_call(
        _kv_kernel,
        grid=(s // tm,),
        in_specs=[pl.BlockSpec((tm, d), lambda i: (i, 0)), _const_spec((d, aw)), _const_spec((aw, d))],
        out_specs=[pl.BlockSpec((npair, r, MOBA_BLOCK, PAIR_W), lambda i: (0, i, 0, 0)),
                   pl.BlockSpec((N_HEADS, r, VT_ROWS, MOBA_BLOCK), lambda i: (0, i, 0, 0)),
                   pl.BlockSpec((r, 1, aw), lambda i: (i, 0, 0))],
        out_shape=[jax.ShapeDtypeStruct((npair, nblk, MOBA_BLOCK, PAIR_W), BF16),
                   jax.ShapeDtypeStruct((N_HEADS, nblk, VT_ROWS, MOBA_BLOCK), BF16),
                   jax.ShapeDtypeStruct((nblk, 1, aw), F32)],
        compiler_params=_params(),
        name="shared_kv",
    )(x, w_k.astype(BF16), w_v.T.astype(BF16))
    return k4, vt4, km.reshape(nblk, aw).astype(BF16)


def _t5_bucket_np(dist):
    n = np.maximum(dist, 0)
    max_exact = REL_BUCKETS // 2
    nf = np.maximum(n, 1).astype(np.float32)
    large = max_exact + (np.log(nf / np.float32(max_exact)) / np.float32(math.log(REL_MAX_DIST / max_exact))
                         * np.float32(REL_BUCKETS - max_exact)).astype(np.int32)
    large = np.minimum(large, REL_BUCKETS - 1)
    return np.where(n < max_exact, n, large).astype(np.int32)


def _bucket_tiles(seq):
    kpos = np.arange(MOBA_BLOCK)[:, None]
    qpos = np.arange(MOBA_BLOCK)[None, :]
    own = _t5_bucket_np(qpos - kpos)
    prev = _t5_bucket_np(qpos - kpos + MOBA_BLOCK)
    far = _t5_bucket_np(np.arange(MOBA_BLOCK + 1, max(seq, MOBA_BLOCK + 2)))
    assert (far == far[0]).all(), "bias must be constant beyond the previous block"
    return own, prev, int(far[0])


def _q_kernel(x_ref, wq_ref, qg_ref, qs_ref):
    q = _dot(x_ref[...].astype(BF16), wq_ref[...]) * np.float32(HEAD_DIM ** -0.5)
    qg_ref[...] = q.astype(BF16)
    qs_ref[...] = (q * np.float32(LOG2E)).astype(BF16)


def _q_proj(x, w_q):
    s, d = x.shape
    aw = w_q.shape[1]
    tm = min(MIX_ROWS, s)
    return pl.pallas_call(
        _q_kernel,
        grid=(s // tm,),
        in_specs=[pl.BlockSpec((tm, d), lambda i: (i, 0)), _const_spec((d, aw))],
        out_specs=[pl.BlockSpec((tm, aw), lambda i: (i, 0))] * 2,
        out_shape=[jax.ShapeDtypeStruct((s, aw), BF16)] * 2,
        compiler_params=_params(),
        name="q_proj",
    )(x, w_q.astype(BF16))


def _moba_kernel(rb_ref, qg_ref, qs_ref, k_ref, vt_ref, km_ref, bo_ref, bp_ref, o_ref, bias_scr, sel_scr,
                 s_scr, *, far_bucket):
    hp = pl.program_id(0)
    i = pl.program_id(1)
    tq = qg_ref.shape[0]
    nblk = km_ref.shape[0]
    heads = range(HEADS_PER_STEP)

    @pl.when(i == 0)
    def _():
        kpos = lax.broadcasted_iota(jnp.int32, (MOBA_BLOCK, tq), 0)
        qpos = lax.broadcasted_iota(jnp.int32, (MOBA_BLOCK, tq), 1)
        for hh in heads:
            h = hp * HEADS_PER_STEP + hh
            for kind, bkt_ref in ((KIND_OWN, bo_ref), (KIND_PREV, bp_ref)):
                bkt = bkt_ref[...]
                tile = jnp.zeros(bkt.shape, F32)
                for bucket in range(REL_BUCKETS):
                    tile = jnp.where(bkt == bucket, rb_ref[h, bucket], tile)
                tile = tile * np.float32(LOG2E)
                if kind == KIND_OWN:
                    tile = jnp.where(qpos >= kpos, tile, NEG_BIG)
                bias_scr[hh, kind] = tile
            bias_scr[hh, KIND_FAR] = jnp.full((MOBA_BLOCK, tq), rb_ref[h, far_bucket] * np.float32(LOG2E), F32)

    lane = lax.broadcasted_iota(jnp.int32, (tq, PAIR_W), 1)
    blk_row = lax.broadcasted_iota(jnp.int32, (nblk, tq), 0)

    def pair_cols(hh):
        pair = hh // PAIR_HEADS
        return slice(pair * PAIR_W, (pair + 1) * PAIR_W)

    def head_only(q_ref, hh):
        lo = (hh % PAIR_HEADS) * HEAD_DIM
        mine = (lane >= lo) & (lane < lo + HEAD_DIM)
        return jnp.where(mine, q_ref[:, pair_cols(hh)].astype(F32), 0.0).astype(BF16)

    qss = [head_only(qs_ref, hh) for hh in heads]

    def scores(step, slot):
        j0 = jnp.minimum(STEP_BLOCKS * step, nblk - STEP_BLOCKS)
        for pair in range(PAIRS_PER_STEP):
            ks = k_ref[pair, pl.ds(j0, STEP_BLOCKS)].reshape(STEP_BLOCKS * MOBA_BLOCK, PAIR_W)
            for hh in range(pair * PAIR_HEADS, (pair + 1) * PAIR_HEADS):
                s_scr[slot, hh] = lax.dot_general(ks, qss[hh], _NT, preferred_element_type=F32)

    gates = [lax.dot_general(km_ref[:, pair_cols(hh)], head_only(qg_ref, hh), _NT, preferred_element_type=F32)
             for hh in heads]
    scores(0, 0)
    for hh in heads:
        gate = jnp.where(blk_row < i, gates[hh], -jnp.inf)
        sel = jnp.where(blk_row == i, 1.0, 0.0)
        for _ in range(MOBA_TOPK):
            top = jnp.max(gate, axis=0, keepdims=True)
            first = jnp.min(jnp.where(gate == top, blk_row, nblk), axis=0, keepdims=True)
            pick = (blk_row == first) & (top > -jnp.inf)
            sel = jnp.where(pick, 1.0, sel)
            gate = jnp.where(pick, -jnp.inf, gate)
        sel_scr[hh] = sel

    c_far = [rb_ref[hp * HEADS_PER_STEP + hh, far_bucket] * np.float32(LOG2E) for hh in heads]

    def consume(step, slot, carries, far_only):
        new = []
        for hh in heads:
            m, acc = carries[hh]
            xs, ons, vts = [], [], []
            shift = c_far[hh] if far_only else 0.0
            m_new = m
            for t in range(STEP_BLOCKS):
                j = STEP_BLOCKS * step + t
                x = s_scr[slot, hh, t * MOBA_BLOCK:(t + 1) * MOBA_BLOCK]
                if far_only:
                    jc = j
                    on = sel_scr[hh, pl.ds(jc, 1), :] > 0.0
                else:
                    jc = jnp.minimum(j, nblk - 1)
                    kind = jnp.where(j == i, KIND_OWN, jnp.where(j == i - 1, KIND_PREV, KIND_FAR))
                    x = x + bias_scr[hh, kind]
                    on = (sel_scr[hh, pl.ds(jc, 1), :] > 0.0) & (j <= i)
                m_new = jnp.maximum(m_new, jnp.where(on, jnp.max(x, axis=0, keepdims=True) + shift, NEG_BIG))
                xs.append(x), ons.append(on), vts.append(vt_ref[hh, jc])
            ps = [jnp.exp2(x - jnp.where(on, m_new - shift, POS_BIG)).astype(BF16) for x, on in zip(xs, ons)]
            pv = _dot(jnp.concatenate(vts, axis=1), jnp.concatenate(ps, axis=0))
            new.append((m_new, acc * jnp.exp2(m - m_new) + pv))
        return tuple(new)

    def trip(t, carries, first_step, steps, far_only):
        base = first_step + steps * t
        for u in range(steps):
            scores(base + u + 1, (u + 1) % 2)
            carries = consume(base + u, u % 2, carries, far_only)
        return carries

    def run(n, carries, **kw):
        return lax.fori_loop(0, n, functools.partial(trip, **kw), carries)

    n_far = jnp.maximum(i - 1, 0) // (2 * STEP_BLOCKS) * 2
    init = (jnp.full((1, tq), NEG_BIG, F32), jnp.zeros((VT_ROWS, tq), F32))
    carries = (init,) * HEADS_PER_STEP
    done = 0
    for steps in TRIP_STEPS:
        n = (n_far - done) // steps
        carries = run(n, carries, first_step=done, steps=steps, far_only=True)
        done = done + n * steps
    n_last = i // STEP_BLOCKS + 1 - n_far
    scores(n_far + 1, 1)
    carries = consume(n_far, 0, carries, False)

    def second(c):
        scores(n_far + 2, 0)
        return consume(n_far + 1, 1, c, False)

    carries = lax.cond(n_last >= 2, second, lambda c: c, carries)
    carries = lax.cond(n_last >= 3, lambda c: consume(n_far + 2, 0, c, False), lambda c: c, carries)
    outs = [acc[:HEAD_DIM] / acc[HEAD_DIM:HEAD_DIM + 1] for _, acc in carries]
    o_ref[...] = jnp.concatenate(outs, axis=0).T.astype(o_ref.dtype)


def _moba_attention(q_gate, q_score, k4, vt4, km, rel_bias):
    s, aw = q_gate.shape
    nblk = s // MOBA_BLOCK
    own, prev, far_bucket = _bucket_tiles(s)
    assert nblk % STEP_BLOCKS == 0 and aw % STEP_W == 0
    tile = pl.BlockSpec((MOBA_BLOCK, STEP_W), lambda hp, i: (i, hp))
    once = dict(pipeline_mode=pl.Buffered(1))
    return pl.pallas_call(
        functools.partial(_moba_kernel, far_bucket=far_bucket),
        grid=(aw // STEP_W, nblk),
        in_specs=[pl.BlockSpec(memory_space=pltpu.SMEM),
                  tile, tile,
                  pl.BlockSpec((PAIRS_PER_STEP, nblk, MOBA_BLOCK, PAIR_W), lambda hp, i: (hp, 0, 0, 0), **once),
                  pl.BlockSpec((HEADS_PER_STEP, nblk, VT_ROWS, MOBA_BLOCK), lambda hp, i: (hp, 0, 0, 0), **once),
                  pl.BlockSpec((nblk, STEP_W), lambda hp, i: (0, hp)),
                  pl.BlockSpec((MOBA_BLOCK, MOBA_BLOCK), lambda hp, i: (0, 0)),
                  pl.BlockSpec((MOBA_BLOCK, MOBA_BLOCK), lambda hp, i: (0, 0))],
        out_specs=tile,
        out_shape=jax.ShapeDtypeStruct((s, aw), BF16),
        scratch_shapes=[pltpu.VMEM((HEADS_PER_STEP, 3, MOBA_BLOCK, MOBA_BLOCK), F32),
                        pltpu.VMEM((HEADS_PER_STEP, nblk, MOBA_BLOCK), F32),
                        pltpu.VMEM((2, HEADS_PER_STEP, STEP_BLOCKS * MOBA_BLOCK, MOBA_BLOCK), F32)],
        compiler_params=_params(2),
        name="moba_attn",
    )(rel_bias, q_gate, q_score, k4, vt4, km, jnp.asarray(own), jnp.asarray(prev))


def _proj_kernel(a_ref, x_ref, w_ref, g_ref, b_ref, o_ref):
    y = DEEPNORM_ALPHA * x_ref[...] + _dot(a_ref[...], w_ref[...])
    o_ref[...] = _layer_norm(y, g_ref[...], b_ref[...])


def _proj_ln(a, x, w, g, b):
    s, d = x.shape
    aw = a.shape[1]
    tm = min(MIX_ROWS, s)
    return pl.pallas_call(
        _proj_kernel,
        grid=(s // tm,),
        in_specs=[pl.BlockSpec((tm, aw), lambda i: (i, 0)), pl.BlockSpec((tm, d), lambda i: (i, 0)),
                  _const_spec((aw, d)), _const_spec((1, d)), _const_spec((1, d))],
        out_specs=pl.BlockSpec((tm, d), lambda i: (i, 0)),
        out_shape=jax.ShapeDtypeStruct((s, d), F32),
        compiler_params=_params(),
        name="attn_out_ln",
    )(a, x, w.astype(BF16), g.reshape(1, d), b.reshape(1, d))


def kernel(x, ln_g, ln_b, ffn_w_gate, ffn_w_up, ffn_w_down, gm_w_in, gm_sgu_g, gm_sgu_b, gm_w_s, gm_b_s, gm_w_out, attn_w_q, attn_w_o, w_k_shared, w_v_shared, rel_bias):
    batch, seq, d = x.shape
    assert seq % MOBA_BLOCK == 0 and w_k_shared.shape[1] == N_HEADS * HEAD_DIM
    rows = x.reshape(batch * seq, d)
    ffn_w = (ffn_w_gate.astype(BF16), ffn_w_up.astype(BF16), ffn_w_down.astype(BF16))
    outs = []
    for bi in range(batch):
        h = rows[bi * seq:(bi + 1) * seq]
        kv = None
        for layer in range(DEPTH):
            h = _ffn_ln(h, *ffn_w, layer, 0, ln_g[layer, 0], ln_b[layer, 0])
            if layer < N_A_LAYERS:
                a = layer
                h = _gmlp_ln(h, gm_w_in[a], gm_sgu_g[a], gm_sgu_b[a], gm_w_s[a], gm_b_s[a], gm_w_out[a],
                             ln_g[layer, 1], ln_b[layer, 1])
            else:
                j = layer - N_A_LAYERS
                o = _moba_attention(*_q_proj(h, attn_w_q[j]), *kv, rel_bias)
                h = _proj_ln(o, h, attn_w_o[j], ln_g[layer, 1], ln_b[layer, 1])
            h = _ffn_ln(h, *ffn_w, layer, 1, ln_g[layer, 2], ln_b[layer, 2])
            if layer == N_A_LAYERS - 1:
                kv = _shared_kv(h, w_k_shared, w_v_shared)
        outs.append(h)
    return jnp.concatenate(outs, axis=0).reshape(batch, seq, d)
```
